```python
import jax, jax.numpy as jnp
from jax import lax
import numpy as np

D_MODEL = 1024
BATCH = 32
SEQ = 2048
DEPTH = 1

D_MIX = D_MODEL
N_SB_HEADS = 8
SB_HEAD_DIM = 64
D_SB = N_SB_HEADS * SB_HEAD_DIM
D_CONV = D_MIX - D_SB
CONV_WIDTH = 31
Q_BLOCK = 128
D_IN_PROJ = 3 * D_SB + 2 * D_CONV

N_MEM = 256
N_XATTN_HEADS = 4
XATTN_HEAD_DIM = D_MODEL // N_XATTN_HEADS

N_GROUPS = 4
EXPERTS_PER_GROUP = 8
N_EXPERTS = N_GROUPS * EXPERTS_PER_GROUP
TOP_K_IN_GROUP = 2
D_EXPERT = 512
MOE_CHUNK = 256

EPS = 1e-6

kernel_name = "hymba_sb_conformer_hmoe_layer"


def rmsnorm(x, g):
    xf = x.astype(jnp.float32)
    y = xf * lax.rsqrt(jnp.mean(xf * xf, axis=-1, keepdims=True) + EPS)
    return (y * g.astype(jnp.float32)).astype(x.dtype)


def stick_breaking_attention(q, k, v):
    S = q.shape[2]
    scale = SB_HEAD_DIM ** -0.5
    outs = []
    for t0 in range(0, S, Q_BLOCK):
        L = t0 + Q_BLOCK
        z = jnp.einsum('bhqd,bhkd->bhqk', q[:, :, t0:L], k[:, :, :L]).astype(jnp.float32) * scale
        q_pos = t0 + jnp.arange(Q_BLOCK)
        k_pos = jnp.arange(L)
        strict = k_pos[None, :] < q_pos[:, None]
        log_keep = jnp.where(strict, jax.nn.log_sigmoid(-z), 0.0)
        later = lax.cumsum(log_keep, axis=3, reverse=True) - log_keep
        a = jnp.where(strict, jnp.exp(jax.nn.log_sigmoid(z) + later), 0.0)
        outs.append(jnp.einsum('bhqk,bhkd->bhqd', a.astype(v.dtype), v[:, :, :L]))
    return jnp.concatenate(outs, axis=2)


def conformer_conv(u, conv_w, conv_b, ln_g, ln_b):
    a, gate = jnp.split(u, 2, axis=-1)
    g = a * jax.nn.sigmoid(gate)
    y = lax.conv_general_dilated(
        g, conv_w[:, None, :], window_strides=(1,), padding=((CONV_WIDTH - 1, 0),),
        dimension_numbers=('NWC', 'WIO', 'NWC'), feature_group_count=D_CONV) + conv_b
    yf = y.astype(jnp.float32)
    mu = jnp.mean(yf, axis=-1, keepdims=True)
    var = jnp.mean(jnp.square(yf - mu), axis=-1, keepdims=True)
    yn = (yf - mu) * lax.rsqrt(var + EPS) * ln_g.astype(jnp.float32) + ln_b.astype(jnp.float32)
    return jax.nn.silu(yn).astype(u.dtype)


def memory_cross_attention(h, mem_n, w_q, w_kv, w_o):
    B, S, _ = h.shape
    M = mem_n.shape[1]
    q = (h @ w_q).reshape(B, S, N_XATTN_HEADS, XATTN_HEAD_DIM)
    kv = (mem_n @ w_kv).reshape(B, M, 2, N_XATTN_HEADS, XATTN_HEAD_DIM)
    k, v = kv[:, :, 0], kv[:, :, 1]
    s = jnp.einsum('bqhd,bkhd->bhqk', q, k).astype(jnp.float32) * (XATTN_HEAD_DIM ** -0.5)
    p = jax.nn.softmax(s, axis=-1).astype(v.dtype)
    o = jnp.einsum('bhqk,bkhd->bqhd', p, v).reshape(B, S, D_MODEL)
    return o @ w_o


def hierarchical_moe(h, w_group, b_group, w_er, b_er, w_gate, w_up, w_down):
    B, S, D = h.shape
    T = B * S
    hf = h.reshape(T, D)
    hr = hf.astype(jnp.float32)
    group_logits = hr @ w_group.astype(jnp.float32) + b_group.astype(jnp.float32)
    group_probs = jax.nn.softmax(group_logits, axis=-1)
    g_idx = jnp.argmax(group_logits, axis=-1).astype(jnp.int32)
    g_w = jnp.take_along_axis(group_probs, g_idx[:, None], axis=1)[:, 0]
    exp_logits = jnp.einsum('td,dge->tge', hr, w_er.astype(jnp.float32)) + b_er.astype(jnp.float32)
    in_group = jnp.take_along_axis(exp_logits, g_idx[:, None, None], axis=1)[:, 0]
    top_v, top_i = lax.top_k(in_group, TOP_K_IN_GROUP)
    weights = g_w[:, None] * jax.nn.softmax(top_v, axis=-1)
    expert_ids = g_idx[:, None] * EXPERTS_PER_GROUP + top_i.astype(jnp.int32)
    N = T * TOP_K_IN_GROUP
    e_flat = expert_ids.reshape(N)
    tok_flat = jnp.arange(N, dtype=jnp.int32) // TOP_K_IN_GROUP
    w_flat = weights.reshape(N)
    order = jnp.argsort(e_flat)
    e_sorted = e_flat[order]
    tok_sorted = tok_flat[order]
    w_sorted = w_flat[order]
    counts = jnp.zeros((N_EXPERTS,), jnp.int32).at[e_flat].add(1)
    padded = (counts + MOE_CHUNK - 1) // MOE_CHUNK * MOE_CHUNK
    start = jnp.cumsum(counts) - counts
    pend = jnp.cumsum(padded)
    pstart = pend - padded
    dest = pstart[e_sorted] + (jnp.arange(N, dtype=jnp.int32) - start[e_sorted])
    n_chunks = (N + MOE_CHUNK - 1) // MOE_CHUNK + N_EXPERTS
    P = n_chunks * MOE_CHUNK
    row_tok = jnp.full((P,), T, jnp.int32).at[dest].set(tok_sorted)
    h_pad = jnp.concatenate([hf, jnp.zeros((1, D), hf.dtype)], axis=0)
    xd = h_pad[row_tok].reshape(n_chunks, MOE_CHUNK, D)
    chunk_e = jnp.minimum(
        jnp.searchsorted(pend, jnp.arange(n_chunks, dtype=jnp.int32) * MOE_CHUNK, side='right'),
        N_EXPERTS - 1).astype(jnp.int32)

    def expert_block(args):
        xc, e = args
        return (jax.nn.silu(xc @ w_gate[e]) * (xc @ w_up[e])) @ w_down[e]

    yd = lax.map(expert_block, (xd, chunk_e)).reshape(P, D)
    y = jnp.zeros((T, D), jnp.float32).at[tok_sorted].add(
        w_sorted[:, None] * yd[dest].astype(jnp.float32))
    return y.astype(h.dtype).reshape(B, S, D)


def setup_inputs(seed: int = 0) -> dict:
    key = jax.random.key(seed)
    ks = jax.random.split(key, 24)
    f32 = jnp.float32
    nrm = lambda k, shape, scale: jax.random.normal(k, shape, f32) * scale
    gain = lambda k, shape: 1.0 + 0.02 * jax.random.normal(k, shape, f32)
    return {
        "x": jax.random.normal(ks[0], (BATCH, SEQ, D_MODEL), f32),
        "mem": jax.random.normal(ks[1], (BATCH, N_MEM, D_MODEL), f32),
        "ln_mix_g": gain(ks[2], (DEPTH, D_MODEL)),
        "w_in": nrm(ks[3], (DEPTH, D_MODEL, D_IN_PROJ), D_MODEL ** -0.5),
        "sb_out_g": gain(ks[4], (DEPTH, D_SB)),
        "conv_w": nrm(ks[5], (DEPTH, CONV_WIDTH, D_CONV), CONV_WIDTH ** -0.5),
        "conv_b": nrm(ks[6], (DEPTH, D_CONV), 0.02),
        "conv_ln_g": gain(ks[7], (DEPTH, D_CONV)),
        "conv_ln_b": nrm(ks[8], (DEPTH, D_CONV), 0.02),
        "w_out": nrm(ks[9], (DEPTH, D_MIX, D_MODEL), D_MIX ** -0.5),
        "ln_mem_x_g": gain(ks[10], (DEPTH, D_MODEL)),
        "ln_mem_g": gain(ks[11], (DEPTH, D_MODEL)),
        "w_xq": nrm(ks[12], (DEPTH, D_MODEL, D_MODEL), D_MODEL ** -0.5),
        "w_xkv": nrm(ks[13], (DEPTH, D_MODEL, 2 * D_MODEL), D_MODEL ** -0.5),
        "w_xo": nrm(ks[14], (DEPTH, D_MODEL, D_MODEL), D_MODEL ** -0.5),
        "ln_ffn_g": gain(ks[15], (DEPTH, D_MODEL)),
        "w_group": nrm(ks[16], (DEPTH, D_MODEL, N_GROUPS), D_MODEL ** -0.5),
        "b_group": nrm(ks[17], (DEPTH, N_GROUPS), 0.01),
        "w_er": nrm(ks[18], (DEPTH, D_MODEL, N_GROUPS, EXPERTS_PER_GROUP), D_MODEL ** -0.5),
        "b_er": nrm(ks[19], (DEPTH, N_GROUPS, EXPERTS_PER_GROUP), 0.01),
        "w_gate": nrm(ks[20], (DEPTH, N_EXPERTS, D_MODEL, D_EXPERT), D_MODEL ** -0.5),
        "w_up": nrm(ks[21], (DEPTH, N_EXPERTS, D_MODEL, D_EXPERT), D_MODEL ** -0.5),
        "w_down": nrm(ks[22], (DEPTH, N_EXPERTS, D_EXPERT, D_MODEL), D_EXPERT ** -0.5),
        "ln_final_g": gain(ks[23], (D_MODEL,)),
    }


def reference(x, mem, ln_mix_g, w_in, sb_out_g, conv_w, conv_b, conv_ln_g, conv_ln_b, w_out,
              ln_mem_x_g, ln_mem_g, w_xq, w_xkv, w_xo, ln_ffn_g, w_group, b_group, w_er, b_er,
              w_gate, w_up, w_down, ln_final_g):
    B, S, _ = x.shape
    for l in range(DEPTH):
        h = rmsnorm(x, ln_mix_g[l])
        proj = h @ w_in[l]
        qkv = proj[..., :3 * D_SB].reshape(B, S, 3, N_SB_HEADS, SB_HEAD_DIM).transpose(2, 0, 3, 1, 4)
        sb = stick_breaking_attention(qkv[0], qkv[1], qkv[2])
        sb = rmsnorm(sb.transpose(0, 2, 1, 3).reshape(B, S, D_SB), sb_out_g[l])
        cv = conformer_conv(proj[..., 3 * D_SB:], conv_w[l], conv_b[l], conv_ln_g[l], conv_ln_b[l])
        x = x + jnp.concatenate([sb, cv], axis=-1) @ w_out[l]
        x = x + memory_cross_attention(rmsnorm(x, ln_mem_x_g[l]), rmsnorm(mem, ln_mem_g[l]),
                                       w_xq[l], w_xkv[l], w_xo[l])
        x = x + hierarchical_moe(rmsnorm(x, ln_ffn_g[l]), w_group[l], b_group[l], w_er[l], b_er[l],
                                 w_gate[l], w_up[l], w_down[l])
    return rmsnorm(x, ln_final_g)
```

```python
import functools

import jax
import jax.numpy as jnp
from jax import lax
from jax.experimental import pallas as pl
from jax.experimental.pallas import tpu as pltpu

F32 = jnp.float32
BF16 = jnp.bfloat16
I32 = jnp.int32
U32 = jnp.uint32

EPS = 1e-6

D_MODEL = 1024
N_SB_HEADS = 8
SB_HEAD_DIM = 64
D_SB = N_SB_HEADS * SB_HEAD_DIM
D_CONV = D_MODEL - D_SB
CONV_WIDTH = 31
N_XHEADS = 4
XHEAD_DIM = D_MODEL // N_XHEADS
N_GROUPS = 4
EXPERTS_PER_GROUP = 8
N_EXPERTS = N_GROUPS * EXPERTS_PER_GROUP
D_EXPERT = 512

LANES = 128
SUBLANES = 8
VMEM_LIMIT = 56 * 1024 * 1024

TM_PROJ = 512
SB_BLK = 256
SB_GROUP = SB_BLK // SUBLANES
TS_CONV = 512
CONV_HALO = 32
CONV_CHUNK = 64
TM_POST = 256
TM_ROWS = 512
TM_EXP = 256
HALF = D_MODEL // 2
RLANE_E = 4
NEG_BIG = -1e30


def _cparams(n_axes):
    return pltpu.CompilerParams(dimension_semantics=("arbitrary",) * n_axes,
                                vmem_limit_bytes=VMEM_LIMIT)


def _rms(x, g):
    ms = jnp.mean(x * x, axis=-1, keepdims=True)
    return x * lax.rsqrt(ms + EPS) * g


def _pack_rows(y):
    lo = pltpu.bitcast(y[:, :HALF].astype(BF16).astype(F32), U32)
    hi = pltpu.bitcast(y[:, HALF:].astype(BF16).astype(F32), U32)
    return (lo >> 16) | (hi & jnp.uint32(0xFFFF0000))


def _unpack_rows(w):
    lo = pltpu.bitcast(w << 16, F32)
    hi = pltpu.bitcast(w & jnp.uint32(0xFFFF0000), F32)
    return lo, hi


def _inproj_kernel(x_ref, g_ref, w_ref, q_ref, k_ref, v_ref, u_ref):
    h = _rms(x_ref[...], g_ref[...]).astype(BF16)
    p = jnp.dot(h, w_ref[...], preferred_element_type=F32)
    q_ref[...] = (p[:, :D_SB] * (SB_HEAD_DIM ** -0.5)).astype(BF16)
    k_ref[...] = p[:, D_SB:2 * D_SB].astype(BF16)
    v_ref[...] = p[:, 2 * D_SB:3 * D_SB].astype(BF16)
    u_ref[...] = p[:, 3 * D_SB:].astype(BF16)


def _inproj(x2d, g, w):
    T = x2d.shape[0]
    n_out = w.shape[1]
    tm = TM_PROJ
    row = lambda width: pl.BlockSpec((tm, width), lambda i: (i, 0))
    return pl.pallas_call(
        _inproj_kernel,
        grid=(T // tm,),
        in_specs=[row(D_MODEL),
                  pl.BlockSpec((1, D_MODEL), lambda i: (0, 0)),
                  pl.BlockSpec((D_MODEL, n_out), lambda i: (0, 0))],
        out_specs=[row(D_SB), row(D_SB), row(D_SB), row(2 * D_CONV)],
        out_shape=[jax.ShapeDtypeStruct((T, D_SB), BF16)] * 3
        + [jax.ShapeDtypeStruct((T, 2 * D_CONV), BF16)],
        compiler_params=_cparams(1),
        name="inproj",
    )(x2d, g.reshape(1, D_MODEL), w)


def _sb_kernel(q_ref, k_ref, vt_ref, g_ref, o_ref, ot_ref):
    qi = pl.program_id(1)
    blk = SB_BLK
    lane = lax.broadcasted_iota(I32, (blk, LANES), 1)
    rowp = lax.broadcasted_iota(I32, (blk, blk), 0)
    colc = lax.broadcasted_iota(I32, (blk, blk), 1)
    key_local = (rowp & (SUBLANES - 1)) * SB_GROUP + (rowp >> 3)
    strict = key_local < colc
    sub = lax.broadcasted_iota(I32, (SUBLANES, blk), 0)

    def pair(j, hh, kb, qm, carry, acc, masked):
        z = lax.dot_general(k_ref[0, j, kb], qm, (((1,), (1,)), ((), ())),
                            preferred_element_type=F32)
        if masked:
            z = jnp.where(strict, z, NEG_BIG)
        sp = jnp.maximum(z, 0.0) + jnp.log(1.0 + jnp.exp(-jnp.abs(z)))
        run = jnp.zeros((SUBLANES, blk), F32)
        es = [None] * SB_GROUP
        for i in reversed(range(SB_GROUP)):
            run = run + sp[SUBLANES * i:SUBLANES * (i + 1)]
            es[i] = jnp.exp(z[SUBLANES * i:SUBLANES * (i + 1)] - run)
        incl = run
        for s in (1, 2, 4):
            shifted = pltpu.roll(incl, SUBLANES - s, 0)
            incl = incl + jnp.where(sub + s < SUBLANES, shifted, 0.0)
        w = jnp.exp(-(carry + (incl - run)))
        at = jnp.concatenate([e * w for e in es], axis=0).astype(BF16)
        vt = vt_ref[0, j, kb, hh * SB_HEAD_DIM:(hh + 1) * SB_HEAD_DIM, :]
        acc = acc + jnp.dot(vt, at, preferred_element_type=F32)
        carry = carry + jnp.broadcast_to(incl[0:1], (SUBLANES, blk))
        return carry, acc

    def head_pair(j, _):
        qs = q_ref[0, j]
        for hh in range(2):
            keep = (lane >= SB_HEAD_DIM) if hh else (lane < SB_HEAD_DIM)
            qm = jnp.where(keep, qs, jnp.zeros_like(qs))
            carry = jnp.zeros((SUBLANES, blk), F32)
            acc = jnp.zeros((SB_HEAD_DIM, blk), F32)
            carry, acc = pair(j, hh, qi, qm, carry, acc, True)

            def body(it, c, hh=hh, qm=qm):
                return pair(j, hh, qi - 1 - it, qm, c[0], c[1], False)

            carry, acc = lax.fori_loop(0, qi, body, (carry, acc))
            ot_ref[j, hh * SB_HEAD_DIM:(hh + 1) * SB_HEAD_DIM, :] = acc
        return 0

    lax.fori_loop(0, N_SB_HEADS // 2, head_pair, 0)
    o = jnp.concatenate([ot_ref[j].T for j in range(N_SB_HEADS // 2)], axis=1)
    o_ref[0] = _rms(o, g_ref[...]).astype(BF16)


def _sb_attention(q4, k5, vt5, g):
    B, npair, S, _ = q4.shape
    nkb = S // SB_BLK
    return pl.pallas_call(
        _sb_kernel,
        grid=(B, nkb),
        in_specs=[pl.BlockSpec((1, npair, SB_BLK, LANES), lambda b, i: (b, 0, i, 0)),
                  pl.BlockSpec((1, npair, nkb, SB_BLK, LANES), lambda b, i: (b, 0, 0, 0, 0)),
                  pl.BlockSpec((1, npair, nkb, LANES, SB_BLK), lambda b, i: (b, 0, 0, 0, 0)),
                  pl.BlockSpec((1, D_SB), lambda b, i: (0, 0))],
        out_specs=pl.BlockSpec((1, SB_BLK, D_SB), lambda b, i: (b, i, 0)),
        out_shape=jax.ShapeDtypeStruct((B, S, D_SB), BF16),
        scratch_shapes=[pltpu.VMEM((npair, LANES, SB_BLK), F32)],
        compiler_params=_cparams(2),
        name="sb_attn",
    )(q4, k5, vt5, g.reshape(1, D_SB))


def _conv_kernel(u_ref, uh_ref, cw_ref, cb_ref, lg_ref, lb_ref, o_ref, g_scr, win_scr):
    i = pl.program_id(1)

    def glu(u):
        a = u[:, :D_CONV].astype(F32)
        gate = u[:, D_CONV:].astype(F32)
        return a * jax.nn.sigmoid(gate)

    gh = glu(uh_ref[0])
    g_scr[0:CONV_HALO, :] = jnp.where(i > 0, gh, 0.0)
    g_scr[CONV_HALO:, :] = glu(u_ref[0])
    first = CONV_HALO - (CONV_WIDTH - 1)

    def chunk(c, _):
        r0 = pl.multiple_of(c * CONV_CHUNK, CONV_CHUNK)
        win_scr[...] = g_scr[pl.ds(r0, CONV_CHUNK + CONV_HALO), :]
        acc = jnp.zeros((CONV_CHUNK, D_CONV), F32) + cb_ref[...]
        for w in range(CONV_WIDTH):
            acc = acc + win_scr[first + w:first + w + CONV_CHUNK, :] * cw_ref[w:w + 1, :]
        mu = jnp.mean(acc, axis=-1, keepdims=True)
        d = acc - mu
        var = jnp.mean(d * d, axis=-1, keepdims=True)
        yn = d * lax.rsqrt(var + EPS) * lg_ref[...] + lb_ref[...]
        o_ref[0, pl.ds(r0, CONV_CHUNK), :] = (yn * jax.nn.sigmoid(yn)).astype(BF16)
        return 0

    lax.fori_loop(0, TS_CONV // CONV_CHUNK, chunk, 0)


def _conv_branch(u3, conv_w, conv_b, ln_g, ln_b):
    B, S, _ = u3.shape
    ts = TS_CONV
    halo_per_step = ts // CONV_HALO
    vec = lambda a: a.reshape(1, D_CONV)
    cw = jnp.zeros((CONV_HALO, D_CONV), F32).at[:CONV_WIDTH].set(conv_w)
    const = lambda rows: pl.BlockSpec((rows, D_CONV), lambda b, i: (0, 0))
    return pl.pallas_call(
        _conv_kernel,
        grid=(B, S // ts),
        in_specs=[pl.BlockSpec((1, ts, 2 * D_CONV), lambda b, i: (b, i, 0)),
                  pl.BlockSpec((1, CONV_HALO, 2 * D_CONV),
                               lambda b, i: (b, jnp.maximum(i * halo_per_step - 1, 0), 0)),
                  const(CONV_HALO), const(1), const(1), const(1)],
        out_specs=pl.BlockSpec((1, ts, D_CONV), lambda b, i: (b, i, 0)),
        out_shape=jax.ShapeDtypeStruct((B, S, D_CONV), BF16),
        scratch_shapes=[pltpu.VMEM((ts + CONV_HALO, D_CONV), F32),
                        pltpu.VMEM((CONV_CHUNK + CONV_HALO, D_CONV), F32)],
        compiler_params=_cparams(2),
        name="conv",
    )(u3, u3, cw, vec(conv_b), vec(ln_g), vec(ln_b))


def _memkv_kernel(m_ref, g_ref, w_ref, kt_ref, v_ref):
    mn = _rms(m_ref[0], g_ref[...]).astype(BF16)
    kv = jnp.dot(mn, w_ref[...], preferred_element_type=F32)
    kt_ref[0] = kv[:, :D_MODEL].T.astype(BF16)
    v_ref[0] = kv[:, D_MODEL:].astype(BF16)


def _memkv(mem, g, w):
    B, M, _ = mem.shape
    return pl.pallas_call(
        _memkv_kernel,
        grid=(B,),
        in_specs=[pl.BlockSpec((1, M, D_MODEL), lambda b: (b, 0, 0)),
                  pl.BlockSpec((1, D_MODEL), lambda b: (0, 0)),
                  pl.BlockSpec((D_MODEL, 2 * D_MODEL), lambda b: (0, 0))],
        out_specs=[pl.BlockSpec((1, D_MODEL, M), lambda b: (b, 0, 0)),
                   pl.BlockSpec((1, M, D_MODEL), lambda b: (b, 0, 0))],
        out_shape=[jax.ShapeDtypeStruct((B, D_MODEL, M), BF16),
                   jax.ShapeDtypeStruct((B, M, D_MODEL), BF16)],
        compiler_params=_cparams(1),
        name="memkv",
    )(mem, g.reshape(1, D_MODEL), w)


def _post_kernel(x_ref, sb_ref, cv_ref, wo_ref, gx_ref, wq_ref, kt_ref, vx_ref, wxo_ref, gf_ref,
                 wr_ref, br_ref, x2_ref, hp_ref, ri_ref, cnt_ref):
    tm = TM_POST
    first_step = jnp.logical_and(pl.program_id(0) == 0, pl.program_id(1) == 0)

    @pl.when(first_step)
    def _():
        cnt_ref[...] = jnp.zeros_like(cnt_ref)

    x1 = (x_ref[0]
          + jnp.dot(sb_ref[0], wo_ref[:D_SB, :], preferred_element_type=F32)
          + jnp.dot(cv_ref[0], wo_ref[D_SB:, :], preferred_element_type=F32))

    hq = _rms(x1, gx_ref[...]).astype(BF16)
    q = (jnp.dot(hq, wq_ref[...], preferred_element_type=F32) * (XHEAD_DIM ** -0.5)).astype(BF16)
    outs = []
    for h in range(N_XHEADS):
        hs = slice(h * XHEAD_DIM, (h + 1) * XHEAD_DIM)
        s = jnp.dot(q[:, hs], kt_ref[0, hs, :], preferred_element_type=F32)
        s = s - jnp.max(s, axis=-1, keepdims=True)
        p = jnp.exp(s)
        p = p / jnp.sum(p, axis=-1, keepdims=True)
        outs.append(jnp.dot(p.astype(BF16), vx_ref[0, :, hs], preferred_element_type=F32))
    o = jnp.concatenate(outs, axis=1).astype(BF16)
    x2 = x1 + jnp.dot(o, wxo_ref[...], preferred_element_type=F32)
    x2_ref[0] = x2

    hn = _rms(x2, gf_ref[...])
    hp_ref[...] = _pack_rows(hn)
    logits = jnp.dot(hn.astype(BF16), wr_ref[...], preferred_element_type=F32) + br_ref[...]
    lane = lax.broadcasted_iota(I32, (tm, LANES), 1)
    big = jnp.int32(LANES)
    ninf = jnp.float32(-jnp.inf)

    def first_argmax(vals):
        m = jnp.max(vals, axis=-1, keepdims=True)
        idx = jnp.min(jnp.where(vals == m, lane, big), axis=-1, keepdims=True)
        return m, idx

    is_group = lane < N_GROUPS
    gl = jnp.where(is_group, logits, ninf)
    gmax, gidx = first_argmax(gl)
    gsum = jnp.sum(jnp.where(is_group, jnp.exp(logits - gmax), 0.0), axis=-1, keepdims=True)
    g_w = 1.0 / gsum
    lo = RLANE_E + gidx * EXPERTS_PER_GROUP
    in_group = jnp.logical_and(lane >= lo, lane < lo + EXPERTS_PER_GROUP)
    el = jnp.where(in_group, logits, ninf)
    v1, i1 = first_argmax(el)
    el2 = jnp.where(lane == i1, ninf, el)
    v2, i2 = first_argmax(el2)
    t = jnp.exp(v2 - v1)
    w1 = g_w / (1.0 + t)
    w2 = g_w * t / (1.0 + t)

    hit1 = lane == i1
    hit2 = lane == i2
    oh = jnp.where(jnp.logical_or(hit1, hit2), 1.0, 0.0)
    rr = lax.broadcasted_iota(I32, (tm, tm), 0)
    cc = lax.broadcasted_iota(I32, (tm, tm), 1)
    ltri = jnp.where(cc < rr, 1.0, 0.0).astype(BF16)
    before = jnp.dot(ltri, oh.astype(BF16), preferred_element_type=F32) + cnt_ref[0:1, :]
    rank1 = jnp.sum(jnp.where(hit1, before, 0.0), axis=-1, keepdims=True)
    rank2 = jnp.sum(jnp.where(hit2, before, 0.0), axis=-1, keepdims=True)
    cnt_ref[...] = cnt_ref[...] + jnp.sum(oh, axis=0, keepdims=True)

    e1 = (i1 - RLANE_E).astype(F32)
    e2 = (i2 - RLANE_E).astype(F32)
    cols = (e1, e2, w1, w2, rank1, rank2)
    info = jnp.zeros((tm, LANES), F32)
    for c, val in enumerate(cols):
        info = jnp.where(lane == c, val, info)
    ri_ref[...] = info


def _post(x, sb, cv, w_out, gx, w_xq, kt, vx, w_xo, gf, w_router, b_router):
    B, S, _ = x.shape
    T = B * S
    tm = TM_POST
    spt = S // tm
    tok = lambda width: pl.BlockSpec((1, tm, width), lambda b, i: (b, i, 0))
    flat = lambda width: pl.BlockSpec((tm, width), lambda b, i: (b * spt + i, 0))
    const = lambda r, c: pl.BlockSpec((r, c), lambda b, i: (0, 0))
    M = vx.shape[1]
    return pl.pallas_call(
        _post_kernel,
        grid=(B, spt),
        in_specs=[tok(D_MODEL), tok(D_SB), tok(D_CONV),
                  const(D_MODEL, D_MODEL), const(1, D_MODEL), const(D_MODEL, D_MODEL),
                  pl.BlockSpec((1, D_MODEL, M), lambda b, i: (b, 0, 0)),
                  pl.BlockSpec((1, M, D_MODEL), lambda b, i: (b, 0, 0)),
                  const(D_MODEL, D_MODEL), const(1, D_MODEL),
                  const(D_MODEL, LANES), const(1, LANES)],
        out_specs=[tok(D_MODEL), flat(HALF), flat(LANES),
                   pl.BlockSpec((SUBLANES, LANES), lambda b, i: (0, 0))],
        out_shape=[jax.ShapeDtypeStruct((B, S, D_MODEL), F32),
                   jax.ShapeDtypeStruct((T, HALF), U32),
                   jax.ShapeDtypeStruct((T, LANES), F32),
                   jax.ShapeDtypeStruct((SUBLANES, LANES), F32)],
        compiler_params=_cparams(2),
        name="post",
    )(x, sb, cv, w_out, gx.reshape(1, D_MODEL), w_xq, kt, vx, w_xo, gf.reshape(1, D_MODEL),
      w_router, b_router)


def _row_copy_wait(src_rows, dst_rows, sem, n):
    pltpu.make_async_copy(src_rows.at[pl.ds(0, n)], dst_rows.at[pl.ds(0, n)], sem).wait()


def _dispatch_kernel(dest_ref, hp_ref, xd_in_ref, xd_ref, idx_ref, sem_idx, sem_rows):
    del xd_in_ref
    tm = TM_ROWS
    i = pl.program_id(0)
    cp = pltpu.make_async_copy(dest_ref.at[i], idx_ref, sem_idx)
    cp.start()
    cp.wait()

    def issue(r, _):
        for k in range(2):
            d = idx_ref[k * tm + r]
            pltpu.make_async_copy(hp_ref.at[pl.ds(r, 1)], xd_ref.at[pl.ds(d, 1)], sem_rows).start()
        return 0

    lax.fori_loop(0, tm, issue, 0, unroll=8)
    for k in range(2):
        _row_copy_wait(hp_ref, xd_ref, sem_rows, tm)


def _dispatch(dest, hp, n_rows):
    T = hp.shape[0]
    tm = TM_ROWS
    xd0 = jnp.zeros((n_rows, HALF), U32)
    return pl.pallas_call(
        _dispatch_kernel,
        grid=(T // tm,),
        in_specs=[pl.BlockSpec(memory_space=pl.ANY),
                  pl.BlockSpec((tm, HALF), lambda i: (i, 0)),
                  pl.BlockSpec(memory_space=pl.ANY)],
        out_specs=pl.BlockSpec(memory_space=pl.ANY),
        out_shape=jax.ShapeDtypeStruct((n_rows, HALF), U32),
        scratch_shapes=[pltpu.SMEM((2 * tm,), I32), pltpu.SemaphoreType.DMA, pltpu.SemaphoreType.DMA],
        input_output_aliases={2: 0},
        compiler_params=_cparams(1),
        name="dispatch",
    )(dest, hp, xd0)


def _experts_kernel(te_ref, tv_ref, xd_ref, wgu_ref, wd_ref, yd_ref):
    del te_ref
    i = pl.program_id(0)

    @pl.when(tv_ref[i] > 0)
    def _():
        lo, hi = _unpack_rows(xd_ref[...])
        gu = (jnp.dot(lo.astype(BF16), wgu_ref[0, :HALF, :], preferred_element_type=F32)
              + jnp.dot(hi.astype(BF16), wgu_ref[0, HALF:, :], preferred_element_type=F32))
        gate = gu[:, :D_EXPERT]
        up = gu[:, D_EXPERT:]
        hmid = (gate * jax.nn.sigmoid(gate) * up).astype(BF16)
        yd_ref[...] = _pack_rows(jnp.dot(hmid, wd_ref[0], preferred_element_type=F32))

    @pl.when(tv_ref[i] == 0)
    def _():
        yd_ref[...] = jnp.zeros_like(yd_ref)


def _experts(tile_expert, tile_valid, xd, wgu, wd):
    n_tiles = tile_expert.shape[0]
    tm = TM_EXP
    grid_spec = pltpu.PrefetchScalarGridSpec(
        num_scalar_prefetch=2,
        grid=(n_tiles,),
        in_specs=[pl.BlockSpec((tm, HALF), lambda i, te, tv: (i, 0)),
                  pl.BlockSpec((1, D_MODEL, 2 * D_EXPERT), lambda i, te, tv: (te[i], 0, 0)),
                  pl.BlockSpec((1, D_EXPERT, D_MODEL), lambda i, te, tv: (te[i], 0, 0))],
        out_specs=pl.BlockSpec((tm, HALF), lambda i, te, tv: (i, 0)),
    )
    return pl.pallas_call(
        _experts_kernel,
        grid_spec=grid_spec,
        out_shape=jax.ShapeDtypeStruct((n_tiles * tm, HALF), U32),
        compiler_params=_cparams(1),
        name="experts",
    )(tile_expert, tile_valid, xd, wgu, wd)


def _combine_kernel(dest_ref, yd_ref, x2_ref, ri_ref, g_ref, o_ref, idx_ref, rows_ref, sem_idx, sem_rows):
    tm = TM_ROWS
    i = pl.program_id(0)
    cp = pltpu.make_async_copy(dest_ref.at[i], idx_ref, sem_idx)
    cp.start()
    cp.wait()

    def issue(r, _):
        for k in range(2):
            d = idx_ref[k * tm + r]
            pltpu.make_async_copy(yd_ref.at[pl.ds(d, 1)], rows_ref.at[k, pl.ds(r, 1)], sem_rows).start()
        return 0

    lax.fori_loop(0, tm, issue, 0, unroll=8)
    for k in range(2):
        _row_copy_wait(yd_ref, rows_ref.at[k], sem_rows, tm)

    ri = ri_ref[...]
    x2 = x2_ref[...]
    y_lo = jnp.zeros((tm, HALF), F32)
    y_hi = jnp.zeros((tm, HALF), F32)
    for k in range(2):
        lo, hi = _unpack_rows(rows_ref[k])
        wk = ri[:, 2 + k:3 + k]
        y_lo = y_lo + wk * lo
        y_hi = y_hi + wk * hi
    x3 = x2 + jnp.concatenate([y_lo, y_hi], axis=1)
    o_ref[...] = _rms(x3, g_ref[...])


def _combine(dest, yd, x2, rinfo, g):
    T = x2.shape[0]
    tm = TM_ROWS
    return pl.pallas_call(
        _combine_kernel,
        grid=(T // tm,),
        in_specs=[pl.BlockSpec(memory_space=pl.ANY),
                  pl.BlockSpec(memory_space=pl.ANY),
                  pl.BlockSpec((tm, D_MODEL), lambda i: (i, 0)),
                  pl.BlockSpec((tm, LANES), lambda i: (i, 0)),
                  pl.BlockSpec((1, D_MODEL), lambda i: (0, 0))],
        out_specs=pl.BlockSpec((tm, D_MODEL), lambda i: (i, 0)),
        out_shape=jax.ShapeDtypeStruct((T, D_MODEL), F32),
        scratch_shapes=[pltpu.SMEM((2 * tm,), I32), pltpu.VMEM((2, tm, HALF), U32),
                        pltpu.SemaphoreType.DMA, pltpu.SemaphoreType.DMA],
        compiler_params=_cparams(1),
        name="combine",
    )(dest, yd, x2, rinfo, g.reshape(1, D_MODEL))


def _sb_layouts(q, k, v, B, S):
    npair = N_SB_HEADS // 2
    nkb = S // SB_BLK
    q4 = q.reshape(B, S, npair, LANES).transpose(0, 2, 1, 3)
    k6 = k.reshape(B, nkb, SUBLANES, SB_GROUP, npair, LANES)
    v6 = v.reshape(B, nkb, SUBLANES, SB_GROUP, npair, LANES)
    k5 = k6.transpose(0, 4, 1, 3, 2, 5).reshape(B, npair, nkb, SB_BLK, LANES)
    vt5 = v6.transpose(0, 4, 1, 5, 3, 2).reshape(B, npair, nkb, LANES, SB_BLK)
    return q4, k5, vt5


def _routing_tables(rinfo, counts_row, T):
    tm = TM_EXP
    counts = counts_row[RLANE_E:RLANE_E + N_EXPERTS].astype(I32)
    padded = (counts + tm - 1) // tm * tm
    pend = jnp.cumsum(padded)
    pstart = pend - padded
    e = rinfo[:, 0:2].astype(I32)
    rank = rinfo[:, 4:6].astype(I32)
    dest = pstart[e] + rank
    n_tiles = (2 * T) // tm + N_EXPERTS
    tile_start = jnp.arange(n_tiles, dtype=I32) * tm
    tile_expert = jnp.minimum(jnp.searchsorted(pend, tile_start, side="right"),
                              N_EXPERTS - 1).astype(I32)
    tile_valid = (tile_start < pend[-1]).astype(I32)
    ntt = T // TM_ROWS
    dest_tiles = dest.reshape(ntt, TM_ROWS, 2).transpose(0, 2, 1).reshape(ntt, 2 * TM_ROWS)
    return dest_tiles, tile_expert, tile_valid, n_tiles


def kernel(x, mem, ln_mix_g, w_in, sb_out_g, conv_w, conv_b, conv_ln_g, conv_ln_b, w_out, ln_mem_x_g,
           ln_mem_g, w_xq, w_xkv, w_xo, ln_ffn_g, w_group, b_group, w_er, b_er, w_gate, w_up, w_down,
           ln_final_g):
    B, S, D = x.shape
    T = B * S
    depth = ln_mix_g.shape[0]
    assert D == D_MODEL and S % TS_CONV == 0 and T % TM_ROWS == 0 and S % SB_BLK == 0
    for l in range(depth):
        q, k, v, u = _inproj(x.reshape(T, D), ln_mix_g[l], w_in[l].astype(BF16))
        q4, k5, vt5 = _sb_layouts(q, k, v, B, S)
        sb = _sb_attention(q4, k5, vt5, sb_out_g[l])
        cv = _conv_branch(u.reshape(B, S, 2 * D_CONV), conv_w[l], conv_b[l], conv_ln_g[l], conv_ln_b[l])
        kt, vx = _memkv(mem, ln_mem_g[l], w_xkv[l].astype(BF16))

        w_router = jnp.zeros((D, LANES), F32)
        w_router = w_router.at[:, :N_GROUPS].set(w_group[l])
        w_router = w_router.at[:, RLANE_E:RLANE_E + N_EXPERTS].set(w_er[l].reshape(D, N_EXPERTS))
        b_router = jnp.zeros((1, LANES), F32)
        b_router = b_router.at[0, :N_GROUPS].set(b_group[l])
        b_router = b_router.at[0, RLANE_E:RLANE_E + N_EXPERTS].set(b_er[l].reshape(N_EXPERTS))

        x2, hp, rinfo, cnt = _post(x, sb, cv, w_out[l].astype(BF16), ln_mem_x_g[l], w_xq[l].astype(BF16),
                                   kt, vx, w_xo[l].astype(BF16), ln_ffn_g[l],
                                   w_router.astype(BF16), b_router)

        dest_tiles, tile_expert, tile_valid, n_tiles = _routing_tables(rinfo, cnt[0], T)
        xd = _dispatch(dest_tiles, hp, n_tiles * TM_EXP)
        wgu = jnp.concatenate([w_gate[l], w_up[l]], axis=-1).astype(BF16)
        yd = _experts(tile_expert, tile_valid, xd, wgu, w_down[l].astype(BF16))
        last = l == depth - 1
        g_out = ln_final_g if last else jnp.ones((D,), F32)
        x = _combine(dest_tiles, yd, x2.reshape(T, D), rinfo, g_out).reshape(B, S, D)
        assert last, "only the final layer applies the output norm inside the combine kernel"
    return x
```

```python
import functools

import jax
import jax.numpy as jnp
from jax import lax
from jax.experimental import pallas as pl
from jax.experimental.pallas import tpu as pltpu

F32 = jnp.float32
BF16 = jnp.bfloat16
I32 = jnp.int32
U32 = jnp.uint32

EPS = 1e-6

D_MODEL = 1024
N_SB_HEADS = 8
SB_HEAD_DIM = 64
D_SB = N_SB_HEADS * SB_HEAD_DIM
D_CONV = D_MODEL - D_SB
CONV_WIDTH = 31
N_XHEADS = 4
XHEAD_DIM = D_MODEL // N_XHEADS
N_GROUPS = 4
EXPERTS_PER_GROUP = 8
N_EXPERTS = N_GROUPS * EXPERTS_PER_GROUP
D_EXPERT = 512

LANES = 128
SUBLANES = 8
VMEM_LIMIT = 56 * 1024 * 1024

TM_PROJ = 512
SB_BLK = 256
SB_GROUP = SB_BLK // SUBLANES
TS_CONV = 512
CONV_HALO = 32
CONV_CHUNK = 32
CONV_L = TS_CONV // SUBLANES
CONV_EXT_ROWS = (CONV_L + CONV_HALO) * SUBLANES
TM_POST = 256
TM_ROWS = 512
TM_EXP = 256
HALF = D_MODEL // 2
RLANE_E = 4
NEG_BIG = -1e30
LOG2E = 1.4426950408889634
SP_LINEAR = 64.0


def _cparams(n_axes):
    return pltpu.CompilerParams(dimension_semantics=("arbitrary",) * n_axes,
                                vmem_limit_bytes=VMEM_LIMIT)


def _rms(x, g):
    ms = jnp.mean(x * x, axis=-1, keepdims=True)
    return x * lax.rsqrt(ms + EPS) * g


def _pack_rows(y):
    lo = pltpu.bitcast(y[:, :HALF].astype(BF16).astype(F32), U32)
    hi = pltpu.bitcast(y[:, HALF:].astype(BF16).astype(F32), U32)
    return (lo >> 16) | (hi & jnp.uint32(0xFFFF0000))


def _unpack_rows(w):
    lo = pltpu.bitcast(w << 16, F32)
    hi = pltpu.bitcast(w & jnp.uint32(0xFFFF0000), F32)
    return lo, hi


def _inproj_kernel(x_ref, g_ref, w_ref, qt_ref, k_ref, vt_ref, u_ref, perm_ref):
    blk = SB_BLK
    npair = N_SB_HEADS // 2

    @pl.when(jnp.logical_and(pl.program_id(0) == 0, pl.program_id(1) == 0))
    def _():
        p = lax.broadcasted_iota(I32, (blk, blk), 0)
        c = lax.broadcasted_iota(I32, (blk, blk), 1)
        perm_ref[...] = jnp.where(c == (p & (SUBLANES - 1)) * SB_GROUP + (p >> 3), 1.0, 0.0).astype(BF16)

    h = _rms(x_ref[0], g_ref[...]).astype(BF16)
    p = jnp.dot(h, w_ref[...], preferred_element_type=F32)
    u_ref[0] = p[:, 3 * D_SB:].astype(BF16)
    q = p[:, :D_SB] * (SB_HEAD_DIM ** -0.5 * LOG2E)
    for j in range(npair):
        qt_ref[0, j] = q[:, j * LANES:(j + 1) * LANES].T.astype(BF16)
    for nb in range(TM_PROJ // blk):
        rows = slice(nb * blk, (nb + 1) * blk)
        kv = p[rows, D_SB:3 * D_SB].astype(BF16)
        kvp = jnp.dot(perm_ref[...], kv, preferred_element_type=F32)
        for j in range(npair):
            k_ref[0, j, nb] = kvp[:, j * LANES:(j + 1) * LANES].astype(BF16)
            vt = kvp[:, D_SB + j * LANES:D_SB + (j + 1) * LANES].T.astype(BF16)
            vt_ref[0, 2 * j, nb] = vt[:SB_HEAD_DIM]
            vt_ref[0, 2 * j + 1, nb] = vt[SB_HEAD_DIM:]


def _inproj(x, g, w):
    B, S, _ = x.shape
    n_out = w.shape[1]
    tm = TM_PROJ
    blk = SB_BLK
    nb = tm // blk
    npair = N_SB_HEADS // 2
    return pl.pallas_call(
        _inproj_kernel,
        grid=(B, S // tm),
        in_specs=[pl.BlockSpec((1, tm, D_MODEL), lambda b, i: (b, i, 0)),
                  pl.BlockSpec((1, D_MODEL), lambda b, i: (0, 0)),
                  pl.BlockSpec((D_MODEL, n_out), lambda b, i: (0, 0))],
        out_specs=[pl.BlockSpec((1, npair, LANES, tm), lambda b, i: (b, 0, 0, i)),
                   pl.BlockSpec((1, npair, nb, blk, LANES), lambda b, i: (b, 0, i, 0, 0)),
                   pl.BlockSpec((1, N_SB_HEADS, nb, SB_HEAD_DIM, blk), lambda b, i: (b, 0, i, 0, 0)),
                   pl.BlockSpec((1, tm, 2 * D_CONV), lambda b, i: (b, i, 0))],
        out_shape=[jax.ShapeDtypeStruct((B, npair, LANES, S), BF16),
                   jax.ShapeDtypeStruct((B, npair, S // blk, blk, LANES), BF16),
                   jax.ShapeDtypeStruct((B, N_SB_HEADS, S // blk, SB_HEAD_DIM, blk), BF16),
                   jax.ShapeDtypeStruct((B, S, 2 * D_CONV), BF16)],
        scratch_shapes=[pltpu.VMEM((blk, blk), BF16)],
        compiler_params=_cparams(2),
        name="inproj",
    )(x, g.reshape(1, D_MODEL), w)


def _sb_kernel(qt_ref, k_ref, vt_ref, g_ref, o_ref, ot_ref, carry_ref, qmt_ref, bias_ref,
               z0, z1, e0, e1, at0, at1):
    qi = pl.program_id(1)
    blk = SB_BLK
    n_items = N_SB_HEADS * (qi + 1)

    @pl.when(jnp.logical_and(pl.program_id(0) == 0, qi == 0))
    def _():
        rowp = lax.broadcasted_iota(I32, (blk, blk), 0)
        colc = lax.broadcasted_iota(I32, (blk, blk), 1)
        key_local = (rowp & (SUBLANES - 1)) * SB_GROUP + (rowp >> 3)
        bias_ref[0] = jnp.zeros((blk, blk), F32)
        bias_ref[1] = jnp.where(key_local < colc, 0.0, NEG_BIG)
        bias_ref[2] = jnp.full((blk, blk), NEG_BIG, F32)

    ot_ref[...] = jnp.zeros_like(ot_ref)
    carry_ref[...] = jnp.zeros_like(carry_ref)
    at1[...] = jnp.zeros_like(at1)
    rowi = lax.broadcasted_iota(I32, (LANES, blk), 0)
    for j in range(N_SB_HEADS // 2):
        qs = qt_ref[0, j]
        zero = jnp.zeros_like(qs)
        qmt_ref[2 * j] = jnp.where(rowi < SB_HEAD_DIM, qs, zero)
        qmt_ref[2 * j + 1] = jnp.where(rowi >= SB_HEAD_DIM, qs, zero)
    sub = lax.broadcasted_iota(I32, (SUBLANES, blk), 0)

    def item(m):
        mc = jnp.minimum(m, n_items - 1)
        return qi - (mc >> 3), mc & (N_SB_HEADS - 1)

    def z_phase(m, zw):
        kb, head = item(m)
        zw[...] = jnp.dot(k_ref[0, head >> 1, kb], qmt_ref[head], preferred_element_type=F32)

    def e_phase(m, zr, er, atw):
        kb, head = item(m)
        bidx = jnp.where(m >= n_items, 2, jnp.where(kb == qi, 1, 0))
        run = jnp.zeros((SUBLANES, blk), F32)
        for i in reversed(range(SB_GROUP)):
            rows = slice(SUBLANES * i, SUBLANES * (i + 1))
            z = zr[rows, :] + bias_ref[bidx, rows, :]
            sp = jnp.where(z > SP_LINEAR, z, jnp.log(1.0 + jnp.exp2(z)) * LOG2E)
            run = run + sp
            er[rows, :] = jnp.exp2(z - run)
        incl = run
        for s in (1, 2, 4):
            shifted = pltpu.roll(incl, SUBLANES - s, 0)
            incl = incl + jnp.where(sub + s < SUBLANES, shifted, 0.0)
        carry = carry_ref[head]
        w = jnp.exp2(-(carry + (incl - run)))
        carry_ref[head] = carry + jnp.broadcast_to(incl[0:1], (SUBLANES, blk))
        w2 = jnp.concatenate([w, w], axis=0)
        for i in range(0, SB_GROUP, 2):
            rows = slice(SUBLANES * i, SUBLANES * (i + 2))
            atw[rows, :] = (er[rows, :] * w2).astype(BF16)

    def av_phase(m, atr):
        kb, head = item(jnp.maximum(m - 1, 0))
        ot_ref[head] = ot_ref[head] + jnp.dot(vt_ref[0, head, kb], atr[...], preferred_element_type=F32)

    z_phase(0, z0)

    def body(t, _):
        m = 2 * t
        z_phase(m + 1, z1)
        e_phase(m, z0, e0, at0)
        av_phase(m, at1)
        z_phase(m + 2, z0)
        e_phase(m + 1, z1, e1, at1)
        av_phase(m + 1, at0)
        return 0

    lax.fori_loop(0, (n_items + 2) // 2, body, 0)
    parts = [ot_ref[2 * j:2 * j + 2].reshape(LANES, blk).T for j in range(N_SB_HEADS // 2)]
    o_ref[0] = _rms(jnp.concatenate(parts, axis=1), g_ref[...]).astype(BF16)


def _sb_attention(qt4, k5, vt5, g):
    B, npair, _, S = qt4.shape
    nkb = S // SB_BLK
    blk = SB_BLK
    return pl.pallas_call(
        _sb_kernel,
        grid=(B, nkb),
        in_specs=[pl.BlockSpec((1, npair, LANES, blk), lambda b, i: (b, 0, 0, i)),
                  pl.BlockSpec((1, npair, nkb, blk, LANES), lambda b, i: (b, 0, 0, 0, 0)),
                  pl.BlockSpec((1, N_SB_HEADS, nkb, SB_HEAD_DIM, blk), lambda b, i: (b, 0, 0, 0, 0)),
                  pl.BlockSpec((1, D_SB), lambda b, i: (0, 0))],
        out_specs=pl.BlockSpec((1, blk, D_SB), lambda b, i: (b, i, 0)),
        out_shape=jax.ShapeDtypeStruct((B, S, D_SB), BF16),
        scratch_shapes=[pltpu.VMEM((N_SB_HEADS, SB_HEAD_DIM, blk), F32),
                        pltpu.VMEM((N_SB_HEADS, SUBLANES, blk), F32),
                        pltpu.VMEM((N_SB_HEADS, LANES, blk), BF16),
                        pltpu.VMEM((3, blk, blk), F32),
                        pltpu.VMEM((blk, blk), F32), pltpu.VMEM((blk, blk), F32),
                        pltpu.VMEM((blk, blk), F32), pltpu.VMEM((blk, blk), F32),
                        pltpu.VMEM((blk, blk), BF16), pltpu.VMEM((blk, blk), BF16)],
        compiler_params=_cparams(2),
        name="sb_attn",
    )(qt4, k5, vt5, g.reshape(1, D_SB))


def _conv_kernel(u_ref, uh_ref, cw_ref, cb_ref, lg_ref, lb_ref, o_ref, pext_ref, pinv_ref, cwb_ref, gext_ref,
                 yp_ref):
    step = pl.program_id(1)
    ts = TS_CONV
    ext = CONV_EXT_ROWS

    @pl.when(jnp.logical_and(pl.program_id(0) == 0, step == 0))
    def _():
        e = lax.broadcasted_iota(I32, (ext, ts + CONV_HALO), 0)
        c = lax.broadcasted_iota(I32, (ext, ts + CONV_HALO), 1)
        src_row = (e & (SUBLANES - 1)) * CONV_L + (e >> 3)
        pext_ref[...] = jnp.where(c == src_row, 1.0, 0.0).astype(BF16)
        t = lax.broadcasted_iota(I32, (ts, ts), 0)
        p = lax.broadcasted_iota(I32, (ts, ts), 1)
        pinv_ref[...] = jnp.where(p == (t & (CONV_L - 1)) * SUBLANES + t // CONV_L, 1.0, 0.0).astype(BF16)
        for w in range(CONV_WIDTH):
            cwb_ref[w] = jnp.broadcast_to(cw_ref[w:w + 1, :], (SUBLANES, D_CONV))

    def glu(u):
        a = u[:, :D_CONV].astype(F32)
        gate = u[:, D_CONV:].astype(F32)
        return a * jax.nn.sigmoid(gate)

    gh = jnp.where(step > 0, glu(uh_ref[0]), 0.0)
    g = jnp.concatenate([gh, glu(u_ref[0])], axis=0).astype(BF16)
    gext_ref[...] = jnp.dot(pext_ref[...], g, preferred_element_type=F32)
    first = CONV_HALO - (CONV_WIDTH - 1)
    rows = CONV_CHUNK

    def chunk(c, _):
        r0 = pl.multiple_of(c * rows, rows)
        acc = jnp.zeros((rows // SUBLANES, SUBLANES, D_CONV), F32)
        for w in range(CONV_WIDTH):
            start = pl.multiple_of(r0 + (first + w) * SUBLANES, SUBLANES)
            gw = gext_ref[pl.ds(start, rows), :].reshape(rows // SUBLANES, SUBLANES, D_CONV)
            acc = acc + gw * cwb_ref[w]
        yp_ref[pl.ds(r0, rows), :] = acc.reshape(rows, D_CONV)
        return 0

    lax.fori_loop(0, ts // rows, chunk, 0)
    y = yp_ref[...] + cb_ref[...]
    mu = jnp.mean(y, axis=-1, keepdims=True)
    d = y - mu
    var = jnp.mean(d * d, axis=-1, keepdims=True)
    yn = d * lax.rsqrt(var + EPS) * lg_ref[...] + lb_ref[...]
    act = (yn * jax.nn.sigmoid(yn)).astype(BF16)
    o_ref[0] = jnp.dot(pinv_ref[...], act, preferred_element_type=F32).astype(BF16)


def _conv_branch(u3, conv_w, conv_b, ln_g, ln_b):
    B, S, _ = u3.shape
    ts = TS_CONV
    halo_per_step = ts // CONV_HALO
    vec = lambda a: a.reshape(1, D_CONV)
    cw = jnp.concatenate([conv_w, jnp.zeros((CONV_HALO - CONV_WIDTH, D_CONV), F32)], axis=0)
    const = lambda rows: pl.BlockSpec((rows, D_CONV), lambda b, i: (0, 0))
    return pl.pallas_call(
        _conv_kernel,
        grid=(B, S // ts),
        in_specs=[pl.BlockSpec((1, ts, 2 * D_CONV), lambda b, i: (b, i, 0)),
                  pl.BlockSpec((1, CONV_HALO, 2 * D_CONV),
                               lambda b, i: (b, jnp.maximum(i * halo_per_step - 1, 0), 0)),
                  const(CONV_HALO), const(1), const(1), const(1)],
        out_specs=pl.BlockSpec((1, ts, D_CONV), lambda b, i: (b, i, 0)),
        out_shape=jax.ShapeDtypeStruct((B, S, D_CONV), BF16),
        scratch_shapes=[pltpu.VMEM((CONV_EXT_ROWS, ts + CONV_HALO), BF16),
                        pltpu.VMEM((ts, ts), BF16),
                        pltpu.VMEM((CONV_HALO, SUBLANES, D_CONV), F32),
                        pltpu.VMEM((CONV_EXT_ROWS, D_CONV), F32),
                        pltpu.VMEM((ts, D_CONV), F32)],
        compiler_params=_cparams(2),
        name="conv",
    )(u3, u3, cw, vec(conv_b), vec(ln_g), vec(ln_b))


def _memkv_kernel(m_ref, g_ref, w_ref, kt_ref, v_ref):
    mn = _rms(m_ref[0], g_ref[...]).astype(BF16)
    kv = jnp.dot(mn, w_ref[...], preferred_element_type=F32)
    kt_ref[0] = kv[:, :D_MODEL].T.astype(BF16)
    v_ref[0] = kv[:, D_MODEL:].astype(BF16)


def _memkv(mem, g, w):
    B, M, _ = mem.shape
    return pl.pallas_call(
        _memkv_kernel,
        grid=(B,),
        in_specs=[pl.BlockSpec((1, M, D_MODEL), lambda b: (b, 0, 0)),
                  pl.BlockSpec((1, D_MODEL), lambda b: (0, 0)),
                  pl.BlockSpec((D_MODEL, 2 * D_MODEL), lambda b: (0, 0))],
        out_specs=[pl.BlockSpec((1, D_MODEL, M), lambda b: (b, 0, 0)),
                   pl.BlockSpec((1, M, D_MODEL), lambda b: (b, 0, 0))],
        out_shape=[jax.ShapeDtypeStruct((B, D_MODEL, M), BF16),
                   jax.ShapeDtypeStruct((B, M, D_MODEL), BF16)],
        compiler_params=_cparams(1),
        name="memkv",
    )(mem, g.reshape(1, D_MODEL), w)


def _post_kernel(x_ref, sb_ref, cv_ref, wo_ref, gx_ref, wq_ref, kt_ref, vx_ref, wxo_ref, gf_ref,
                 wr_ref, br_ref, x2_ref, hp_ref, ri_ref, cnt_ref):
    tm = TM_POST
    first_step = jnp.logical_and(pl.program_id(0) == 0, pl.program_id(1) == 0)

    @pl.when(first_step)
    def _():
        cnt_ref[...] = jnp.zeros_like(cnt_ref)

    x1 = (x_ref[0]
          + jnp.dot(sb_ref[0], wo_ref[:D_SB, :], preferred_element_type=F32)
          + jnp.dot(cv_ref[0], wo_ref[D_SB:, :], preferred_element_type=F32))

    hq = _rms(x1, gx_ref[...]).astype(BF16)
    q = (jnp.dot(hq, wq_ref[...], preferred_element_type=F32) * (XHEAD_DIM ** -0.5)).astype(BF16)
    outs = []
    for h in range(N_XHEADS):
        hs = slice(h * XHEAD_DIM, (h + 1) * XHEAD_DIM)
        s = jnp.dot(q[:, hs], kt_ref[0, hs, :], preferred_element_type=F32)
        s = s - jnp.max(s, axis=-1, keepdims=True)
        p = jnp.exp(s)
        p = p / jnp.sum(p, axis=-1, keepdims=True)
        outs.append(jnp.dot(p.astype(BF16), vx_ref[0, :, hs], preferred_element_type=F32))
    o = jnp.concatenate(outs, axis=1).astype(BF16)
    x2 = x1 + jnp.dot(o, wxo_ref[...], preferred_element_type=F32)
    x2_ref[0] = x2

    hn = _rms(x2, gf_ref[...])
    hp_ref[...] = _pack_rows(hn)
    logits = jnp.dot(hn.astype(BF16), wr_ref[...], preferred_element_type=F32) + br_ref[...]
    lane = lax.broadcasted_iota(I32, (tm, LANES), 1)
    big = jnp.int32(LANES)
    ninf = jnp.float32(-jnp.inf)

    def first_argmax(vals):
        m = jnp.max(vals, axis=-1, keepdims=True)
        idx = jnp.min(jnp.where(vals == m, lane, big), axis=-1, keepdims=True)
        return m, idx

    is_group = lane < N_GROUPS
    gl = jnp.where(is_group, logits, ninf)
    gmax, gidx = first_argmax(gl)
    gsum = jnp.sum(jnp.where(is_group, jnp.exp(logits - gmax), 0.0), axis=-1, keepdims=True)
    g_w = 1.0 / gsum
    lo = RLANE_E + gidx * EXPERTS_PER_GROUP
    in_group = jnp.logical_and(lane >= lo, lane < lo + EXPERTS_PER_GROUP)
    el = jnp.where(in_group, logits, ninf)
    v1, i1 = first_argmax(el)
    el2 = jnp.where(lane == i1, ninf, el)
    v2, i2 = first_argmax(el2)
    t = jnp.exp(v2 - v1)
    w1 = g_w / (1.0 + t)
    w2 = g_w * t / (1.0 + t)

    hit1 = lane == i1
    hit2 = lane == i2
    oh = jnp.where(jnp.logical_or(hit1, hit2), 1.0, 0.0)
    rr = lax.broadcasted_iota(I32, (tm, tm), 0)
    cc = lax.broadcasted_iota(I32, (tm, tm), 1)
    ltri = jnp.where(cc < rr, 1.0, 0.0).astype(BF16)
    before = jnp.dot(ltri, oh.astype(BF16), preferred_element_type=F32) + cnt_ref[0:1, :]
    rank1 = jnp.sum(jnp.where(hit1, before, 0.0), axis=-1, keepdims=True)
    rank2 = jnp.sum(jnp.where(hit2, before, 0.0), axis=-1, keepdims=True)
    cnt_ref[...] = cnt_ref[...] + jnp.sum(oh, axis=0, keepdims=True)

    e1 = (i1 - RLANE_E).astype(F32)
    e2 = (i2 - RLANE_E).astype(F32)
    cols = (e1, e2, w1, w2, rank1, rank2)
    info = jnp.zeros((tm, LANES), F32)
    for c, val in enumerate(cols):
        info = jnp.where(lane == c, val, info)
    ri_ref[...] = info


def _post(x, sb, cv, w_out, gx, w_xq, kt, vx, w_xo, gf, w_router, b_router):
    B, S, _ = x.shape
    T = B * S
    tm = TM_POST
    spt = S // tm
    tok = lambda width: pl.BlockSpec((1, tm, width), lambda b, i: (b, i, 0))
    flat = lambda width: pl.BlockSpec((tm, width), lambda b, i: (b * spt + i, 0))
    const = lambda r, c: pl.BlockSpec((r, c), lambda b, i: (0, 0))
    M = vx.shape[1]
    return pl.pallas_call(
        _post_kernel,
        grid=(B, spt),
        in_specs=[tok(D_MODEL), tok(D_SB), tok(D_CONV),
                  const(D_MODEL, D_MODEL), const(1, D_MODEL), const(D_MODEL, D_MODEL),
                  pl.BlockSpec((1, D_MODEL, M), lambda b, i: (b, 0, 0)),
                  pl.BlockSpec((1, M, D_MODEL), lambda b, i: (b, 0, 0)),
                  const(D_MODEL, D_MODEL), const(1, D_MODEL),
                  const(D_MODEL, LANES), const(1, LANES)],
        out_specs=[tok(D_MODEL), flat(HALF), flat(LANES),
                   pl.BlockSpec((SUBLANES, LANES), lambda b, i: (0, 0))],
        out_shape=[jax.ShapeDtypeStruct((B, S, D_MODEL), F32),
                   jax.ShapeDtypeStruct((T, HALF), U32),
                   jax.ShapeDtypeStruct((T, LANES), F32),
                   jax.ShapeDtypeStruct((SUBLANES, LANES), F32)],
        compiler_params=_cparams(2),
        name="post",
    )(x, sb, cv, w_out, gx.reshape(1, D_MODEL), w_xq, kt, vx, w_xo, gf.reshape(1, D_MODEL),
      w_router, b_router)


def _row_copy_wait(src_rows, dst_rows, sem, n):
    pltpu.make_async_copy(src_rows.at[pl.ds(0, n)], dst_rows.at[pl.ds(0, n)], sem).wait()


def _dispatch_kernel(dest_ref, hp_ref, xd_in_ref, xd_ref, idx_ref, sem_idx, sem_rows):
    del xd_in_ref
    tm = TM_ROWS
    i = pl.program_id(0)
    cp = pltpu.make_async_copy(dest_ref.at[i], idx_ref, sem_idx)
    cp.start()
    cp.wait()

    def issue(r, _):
        for k in range(2):
            d = idx_ref[k * tm + r]
            pltpu.make_async_copy(hp_ref.at[pl.ds(r, 1)], xd_ref.at[pl.ds(d, 1)], sem_rows).start()
        return 0

    lax.fori_loop(0, tm, issue, 0, unroll=8)
    for k in range(2):
        _row_copy_wait(hp_ref, xd_ref, sem_rows, tm)


def _dispatch(dest, hp, n_rows):
    T = hp.shape[0]
    tm = TM_ROWS
    xd0 = jnp.zeros((n_rows, HALF), U32)
    return pl.pallas_call(
        _dispatch_kernel,
        grid=(T // tm,),
        in_specs=[pl.BlockSpec(memory_space=pl.ANY),
                  pl.BlockSpec((tm, HALF), lambda i: (i, 0)),
                  pl.BlockSpec(memory_space=pl.ANY)],
        out_specs=pl.BlockSpec(memory_space=pl.ANY),
        out_shape=jax.ShapeDtypeStruct((n_rows, HALF), U32),
        scratch_shapes=[pltpu.SMEM((2 * tm,), I32), pltpu.SemaphoreType.DMA, pltpu.SemaphoreType.DMA],
        input_output_aliases={2: 0},
        compiler_params=_cparams(1),
        name="dispatch",
    )(dest, hp, xd0)


def _experts_kernel(te_ref, tv_ref, xd_ref, wgu_ref, wd_ref, yd_ref):
    del te_ref
    i = pl.program_id(0)

    @pl.when(tv_ref[i] > 0)
    def _():
        lo, hi = _unpack_rows(xd_ref[...])
        gu = (jnp.dot(lo.astype(BF16), wgu_ref[0, :HALF, :], preferred_element_type=F32)
              + jnp.dot(hi.astype(BF16), wgu_ref[0, HALF:, :], preferred_element_type=F32))
        gate = gu[:, :D_EXPERT]
        up = gu[:, D_EXPERT:]
        hmid = (gate * jax.nn.sigmoid(gate) * up).astype(BF16)
        yd_ref[...] = _pack_rows(jnp.dot(hmid, wd_ref[0], preferred_element_type=F32))

    @pl.when(tv_ref[i] == 0)
    def _():
        yd_ref[...] = jnp.zeros_like(yd_ref)


def _experts(tile_expert, tile_valid, xd, wgu, wd):
    n_tiles = tile_expert.shape[0]
    tm = TM_EXP
    grid_spec = pltpu.PrefetchScalarGridSpec(
        num_scalar_prefetch=2,
        grid=(n_tiles,),
        in_specs=[pl.BlockSpec((tm, HALF), lambda i, te, tv: (i, 0)),
                  pl.BlockSpec((1, D_MODEL, 2 * D_EXPERT), lambda i, te, tv: (te[i], 0, 0)),
                  pl.BlockSpec((1, D_EXPERT, D_MODEL), lambda i, te, tv: (te[i], 0, 0))],
        out_specs=pl.BlockSpec((tm, HALF), lambda i, te, tv: (i, 0)),
    )
    return pl.pallas_call(
        _experts_kernel,
        grid_spec=grid_spec,
        out_shape=jax.ShapeDtypeStruct((n_tiles * tm, HALF), U32),
        compiler_params=_cparams(1),
        name="experts",
    )(tile_expert, tile_valid, xd, wgu, wd)


def _combine_kernel(dest_ref, yd_ref, x2_ref, ri_ref, g_ref, o_ref, idx_ref, rows_ref, sem_idx, sem_rows):
    tm = TM_ROWS
    i = pl.program_id(0)
    cp = pltpu.make_async_copy(dest_ref.at[i], idx_ref, sem_idx)
    cp.start()
    cp.wait()

    def issue(r, _):
        for k in range(2):
            d = idx_ref[k * tm + r]
            pltpu.make_async_copy(yd_ref.at[pl.ds(d, 1)], rows_ref.at[k, pl.ds(r, 1)], sem_rows).start()
        return 0

    lax.fori_loop(0, tm, issue, 0, unroll=8)
    for k in range(2):
        _row_copy_wait(yd_ref, rows_ref.at[k], sem_rows, tm)

    ri = ri_ref[...]
    x2 = x2_ref[...]
    y_lo = jnp.zeros((tm, HALF), F32)
    y_hi = jnp.zeros((tm, HALF), F32)
    for k in range(2):
        lo, hi = _unpack_rows(rows_ref[k])
        wk = ri[:, 2 + k:3 + k]
        y_lo = y_lo + wk * lo
        y_hi = y_hi + wk * hi
    x3 = x2 + jnp.concatenate([y_lo, y_hi], axis=1)
    o_ref[...] = _rms(x3, g_ref[...])


def _combine(dest, yd, x2, rinfo, g):
    T = x2.shape[0]
    tm = TM_ROWS
    return pl.pallas_call(
        _combine_kernel,
        grid=(T // tm,),
        in_specs=[pl.BlockSpec(memory_space=pl.ANY),
                  pl.BlockSpec(memory_space=pl.ANY),
                  pl.BlockSpec((tm, D_MODEL), lambda i: (i, 0)),
                  pl.BlockSpec((tm, LANES), lambda i: (i, 0)),
                  pl.BlockSpec((1, D_MODEL), lambda i: (0, 0))],
        out_specs=pl.BlockSpec((tm, D_MODEL), lambda i: (i, 0)),
        out_shape=jax.ShapeDtypeStruct((T, D_MODEL), F32),
        scratch_shapes=[pltpu.SMEM((2 * tm,), I32), pltpu.VMEM((2, tm, HALF), U32),
                        pltpu.SemaphoreType.DMA, pltpu.SemaphoreType.DMA],
        compiler_params=_cparams(1),
        name="combine",
    )(dest, yd, x2, rinfo, g.reshape(1, D_MODEL))


def _routing_tables(rinfo, counts_row, T):
    tm = TM_EXP
    counts = counts_row[RLANE_E:RLANE_E + N_EXPERTS].astype(I32)
    padded = (counts + tm - 1) // tm * tm
    pend = jnp.cumsum(padded)
    pstart = pend - padded
    e = rinfo[:, 0:2].astype(I32)
    rank = rinfo[:, 4:6].astype(I32)
    ids = jnp.arange(N_EXPERTS, dtype=I32)
    dest = rank + jnp.sum(jnp.where(e[..., None] == ids, pstart, 0), axis=-1)
    n_tiles = (2 * T) // tm + N_EXPERTS
    tile_start = jnp.arange(n_tiles, dtype=I32) * tm
    tile_expert = jnp.minimum(jnp.sum((tile_start[:, None] >= pend[None, :]).astype(I32), axis=1),
                              N_EXPERTS - 1)
    tile_valid = (tile_start < pend[-1]).astype(I32)
    ntt = T // TM_ROWS
    dest_tiles = dest.reshape(ntt, TM_ROWS, 2).transpose(0, 2, 1).reshape(ntt, 2 * TM_ROWS)
    return dest_tiles, tile_expert, tile_valid, n_tiles


def kernel(x, mem, ln_mix_g, w_in, sb_out_g, conv_w, conv_b, conv_ln_g, conv_ln_b, w_out, ln_mem_x_g,
           ln_mem_g, w_xq, w_xkv, w_xo, ln_ffn_g, w_group, b_group, w_er, b_er, w_gate, w_up, w_down,
           ln_final_g):
    B, S, D = x.shape
    T = B * S
    assert ln_mix_g.shape[0] == 1, "single-layer pipeline: the output norm is fused into the combine kernel"
    assert D == D_MODEL and S % TS_CONV == 0 and S % TM_PROJ == 0 and T % TM_ROWS == 0
    l = 0
    qt4, k5, vt5, u = _inproj(x, ln_mix_g[l], w_in[l].astype(BF16))
    sb = _sb_attention(qt4, k5, vt5, sb_out_g[l])
    cv = _conv_branch(u, conv_w[l], conv_b[l], conv_ln_g[l], conv_ln_b[l])
    kt, vx = _memkv(mem, ln_mem_g[l], w_xkv[l].astype(BF16))

    pad = LANES - RLANE_E - N_EXPERTS
    w_router = jnp.concatenate([w_group[l], w_er[l].reshape(D, N_EXPERTS), jnp.zeros((D, pad), F32)], axis=1)
    b_router = jnp.concatenate([b_group[l], b_er[l].reshape(N_EXPERTS), jnp.zeros((pad,), F32)])
    x2, hp, rinfo, cnt = _post(x, sb, cv, w_out[l].astype(BF16), ln_mem_x_g[l], w_xq[l].astype(BF16),
                               kt, vx, w_xo[l].astype(BF16), ln_ffn_g[l],
                               w_router.astype(BF16), b_router.reshape(1, LANES))

    dest_tiles, tile_expert, tile_valid, n_tiles = _routing_tables(rinfo, cnt[0], T)
    xd = _dispatch(dest_tiles, hp, n_tiles * TM_EXP)
    wgu = jnp.concatenate([w_gate[l], w_up[l]], axis=-1).astype(BF16)
    yd = _experts(tile_expert, tile_valid, xd, wgu, w_down[l].astype(BF16))
    return _combine(dest_tiles, yd, x2.reshape(T, D), rinfo, ln_final_g).reshape(B, S, D)
```

```python
import functools

import jax
import jax.numpy as jnp
from jax import lax
from jax.experimental import pallas as pl
from jax.experimental.pallas import tpu as pltpu

F32 = jnp.float32
BF16 = jnp.bfloat16
I32 = jnp.int32
U32 = jnp.uint32

EPS = 1e-6

D_MODEL = 1024
N_SB_HEADS = 8
SB_HEAD_DIM = 64
D_SB = N_SB_HEADS * SB_HEAD_DIM
D_CONV = D_MODEL - D_SB
CONV_WIDTH = 31
N_XHEADS = 4
XHEAD_DIM = D_MODEL // N_XHEADS
N_GROUPS = 4
EXPERTS_PER_GROUP = 8
N_EXPERTS = N_GROUPS * EXPERTS_PER_GROUP
D_EXPERT = 512

LANES = 128
SUBLANES = 8
VMEM_LIMIT = 56 * 1024 * 1024

TM_PROJ = 512
SB_BLK = 256
SB_GROUP = SB_BLK // SUBLANES
TS_CONV = 512
CONV_HALO = 32
CONV_CHUNK = 32
CONV_L = TS_CONV // SUBLANES
CONV_EXT_ROWS = (CONV_L + CONV_HALO) * SUBLANES
TM_POST = 512
POST_SPLIT = 2
TM_ROWS = 512
TM_EXP = 512
EXP_SPLIT = 2
ISSUE_BATCH = 8
HALF = D_MODEL // 2
ROW_CHUNKS = HALF // LANES
RLANE_E = 4
NEG_BIG = -1e30
LOG2E = 1.4426950408889634
LOGIT_CAP = 64.0


def _cparams(n_axes):
    return pltpu.CompilerParams(dimension_semantics=("arbitrary",) * n_axes,
                                vmem_limit_bytes=VMEM_LIMIT)


def _rms(x, g):
    ms = jnp.mean(x * x, axis=-1, keepdims=True)
    return x * lax.rsqrt(ms + EPS) * g


def _pack_rows(y):
    lo = pltpu.bitcast(y[:, :HALF].astype(BF16).astype(F32), U32)
    hi = pltpu.bitcast(y[:, HALF:].astype(BF16).astype(F32), U32)
    return (lo >> 16) | (hi & jnp.uint32(0xFFFF0000))


def _unpack_rows(w):
    lo = pltpu.bitcast(w << 16, F32)
    hi = pltpu.bitcast(w & jnp.uint32(0xFFFF0000), F32)
    return lo, hi


def _store_token_rows(ref, first, words):
    rows = words.shape[0]
    for c in range(ROW_CHUNKS):
        ref[pl.ds(first * ROW_CHUNKS + c, rows, stride=ROW_CHUNKS), :] = words[:, c * LANES:(c + 1) * LANES]


def _load_token_rows(ref, first, rows):
    return jnp.concatenate([ref[pl.ds(first * ROW_CHUNKS + c, rows, stride=ROW_CHUNKS), :]
                            for c in range(ROW_CHUNKS)], axis=1)


def _inproj_kernel(x_ref, g_ref, w_ref, qt_ref, k_ref, vt_ref, u_ref, perm_ref):
    blk = SB_BLK
    npair = N_SB_HEADS // 2

    @pl.when(jnp.logical_and(pl.program_id(0) == 0, pl.program_id(1) == 0))
    def _():
        p = lax.broadcasted_iota(I32, (blk, blk), 0)
        c = lax.broadcasted_iota(I32, (blk, blk), 1)
        perm_ref[...] = jnp.where(c == (p & (SUBLANES - 1)) * SB_GROUP + (p >> 3), 1.0, 0.0).astype(BF16)

    h = _rms(x_ref[0], g_ref[...]).astype(BF16)
    p = jnp.dot(h, w_ref[...], preferred_element_type=F32)
    u_ref[0] = p[:, 3 * D_SB:].astype(BF16)
    q = p[:, :D_SB] * (SB_HEAD_DIM ** -0.5 * LOG2E)
    for j in range(npair):
        qt_ref[0, j] = q[:, j * LANES:(j + 1) * LANES].T.astype(BF16)
    for nb in range(TM_PROJ // blk):
        rows = slice(nb * blk, (nb + 1) * blk)
        kv = p[rows, D_SB:3 * D_SB].astype(BF16)
        kvp = jnp.dot(perm_ref[...], kv, preferred_element_type=F32)
        for j in range(npair):
            k_ref[0, j, nb] = kvp[:, j * LANES:(j + 1) * LANES].astype(BF16)
            vt = kvp[:, D_SB + j * LANES:D_SB + (j + 1) * LANES].T.astype(BF16)
            vt_ref[0, 2 * j, nb] = vt[:SB_HEAD_DIM]
            vt_ref[0, 2 * j + 1, nb] = vt[SB_HEAD_DIM:]


def _inproj(x, g, w):
    B, S, _ = x.shape
    n_out = w.shape[1]
    tm = TM_PROJ
    blk = SB_BLK
    nb = tm // blk
    npair = N_SB_HEADS // 2
    return pl.pallas_call(
        _inproj_kernel,
        grid=(B, S // tm),
        in_specs=[pl.BlockSpec((1, tm, D_MODEL), lambda b, i: (b, i, 0)),
                  pl.BlockSpec((1, D_MODEL), lambda b, i: (0, 0)),
                  pl.BlockSpec((D_MODEL, n_out), lambda b, i: (0, 0))],
        out_specs=[pl.BlockSpec((1, npair, LANES, tm), lambda b, i: (b, 0, 0, i)),
                   pl.BlockSpec((1, npair, nb, blk, LANES), lambda b, i: (b, 0, i, 0, 0)),
                   pl.BlockSpec((1, N_SB_HEADS, nb, SB_HEAD_DIM, blk), lambda b, i: (b, 0, i, 0, 0)),
                   pl.BlockSpec((1, tm, 2 * D_CONV), lambda b, i: (b, i, 0))],
        out_shape=[jax.ShapeDtypeStruct((B, npair, LANES, S), BF16),
                   jax.ShapeDtypeStruct((B, npair, S // blk, blk, LANES), BF16),
                   jax.ShapeDtypeStruct((B, N_SB_HEADS, S // blk, SB_HEAD_DIM, blk), BF16),
                   jax.ShapeDtypeStruct((B, S, 2 * D_CONV), BF16)],
        scratch_shapes=[pltpu.VMEM((blk, blk), BF16)],
        compiler_params=_cparams(2),
        name="inproj",
    )(x, g.reshape(1, D_MODEL), w)


def _sb_kernel(qt_ref, k_ref, vt_ref, g_ref, o_ref, ot_ref, carry_ref, qmt_ref, bias_ref,
               z0, z1, e0, e1, at0, at1):
    qi = pl.program_id(1)
    blk = SB_BLK
    n_items = N_SB_HEADS * (qi + 1)

    @pl.when(jnp.logical_and(pl.program_id(0) == 0, qi == 0))
    def _():
        rowp = lax.broadcasted_iota(I32, (blk, blk), 0)
        colc = lax.broadcasted_iota(I32, (blk, blk), 1)
        key_local = (rowp & (SUBLANES - 1)) * SB_GROUP + (rowp >> 3)
        bias_ref[0] = jnp.zeros((blk, blk), F32)
        bias_ref[1] = jnp.where(key_local < colc, 0.0, NEG_BIG)
        bias_ref[2] = jnp.full((blk, blk), NEG_BIG, F32)

    ot_ref[...] = jnp.zeros_like(ot_ref)
    carry_ref[...] = jnp.ones_like(carry_ref)
    at1[...] = jnp.zeros_like(at1)
    rowi = lax.broadcasted_iota(I32, (LANES, blk), 0)
    for j in range(N_SB_HEADS // 2):
        qs = qt_ref[0, j]
        zero = jnp.zeros_like(qs)
        qmt_ref[2 * j] = jnp.where(rowi < SB_HEAD_DIM, qs, zero)
        qmt_ref[2 * j + 1] = jnp.where(rowi >= SB_HEAD_DIM, qs, zero)
    sub = lax.broadcasted_iota(I32, (SUBLANES, blk), 0)

    def item(m):
        mc = jnp.minimum(m, n_items - 1)
        return qi - (mc >> 3), mc & (N_SB_HEADS - 1)

    def z_phase(m, zw):
        kb, head = item(m)
        zw[...] = jnp.dot(k_ref[0, head >> 1, kb], qmt_ref[head], preferred_element_type=F32)

    def e_phase(m, zr, er, atw):
        kb, head = item(m)
        bidx = jnp.where(m >= n_items, 2, jnp.where(kb == qi, 1, 0))
        pex = jnp.ones((SUBLANES, blk), F32)
        for i in reversed(range(SB_GROUP)):
            rows = slice(SUBLANES * i, SUBLANES * (i + 1))
            z = zr[rows, :] + bias_ref[bidx, rows, :]
            t = jnp.exp2(jnp.minimum(z, LOGIT_CAP))
            keep = 1.0 / (1.0 + t)
            er[rows, :] = (t * keep) * pex
            pex = pex * keep
        incl = pex
        for s in (1, 2, 4):
            shifted = pltpu.roll(incl, SUBLANES - s, 0)
            incl = incl * jnp.where(sub + s < SUBLANES, shifted, 1.0)
        later = jnp.where(sub + 1 < SUBLANES, pltpu.roll(incl, SUBLANES - 1, 0), 1.0)
        carry = carry_ref[head]
        w = carry * later
        carry_ref[head] = carry * jnp.broadcast_to(incl[0:1], (SUBLANES, blk))
        w2 = jnp.concatenate([w, w], axis=0)
        for i in range(0, SB_GROUP, 2):
            rows = slice(SUBLANES * i, SUBLANES * (i + 2))
            atw[rows, :] = (er[rows, :] * w2).astype(BF16)

    def av_phase(m, atr):
        kb, head = item(jnp.maximum(m - 1, 0))
        ot_ref[head] = ot_ref[head] + jnp.dot(vt_ref[0, head, kb], atr[...], preferred_element_type=F32)

    z_phase(0, z0)

    def body(t, _):
        m = 2 * t
        z_phase(m + 1, z1)
        e_phase(m, z0, e0, at0)
        av_phase(m, at1)
        z_phase(m + 2, z0)
        e_phase(m + 1, z1, e1, at1)
        av_phase(m + 1, at0)
        return 0

    lax.fori_loop(0, (n_items + 2) // 2, body, 0)
    parts = [ot_ref[2 * j:2 * j + 2].reshape(LANES, blk).T for j in range(N_SB_HEADS // 2)]
    o_ref[0] = _rms(jnp.concatenate(parts, axis=1), g_ref[...]).astype(BF16)


def _sb_attention(qt4, k5, vt5, g):
    B, npair, _, S = qt4.shape
    nkb = S // SB_BLK
    blk = SB_BLK
    return pl.pallas_call(
        _sb_kernel,
        grid=(B, nkb),
        in_specs=[pl.BlockSpec((1, npair, LANES, blk), lambda b, i: (b, 0, 0, i)),
                  pl.BlockSpec((1, npair, nkb, blk, LANES), lambda b, i: (b, 0, 0, 0, 0)),
                  pl.BlockSpec((1, N_SB_HEADS, nkb, SB_HEAD_DIM, blk), lambda b, i: (b, 0, 0, 0, 0)),
                  pl.BlockSpec((1, D_SB), lambda b, i: (0, 0))],
        out_specs=pl.BlockSpec((1, blk, D_SB), lambda b, i: (b, i, 0)),
        out_shape=jax.ShapeDtypeStruct((B, S, D_SB), BF16),
        scratch_shapes=[pltpu.VMEM((N_SB_HEADS, SB_HEAD_DIM, blk), F32),
                        pltpu.VMEM((N_SB_HEADS, SUBLANES, blk), F32),
                        pltpu.VMEM((N_SB_HEADS, LANES, blk), BF16),
                        pltpu.VMEM((3, blk, blk), F32),
                        pltpu.VMEM((blk, blk), F32), pltpu.VMEM((blk, blk), F32),
                        pltpu.VMEM((blk, blk), F32), pltpu.VMEM((blk, blk), F32),
                        pltpu.VMEM((blk, blk), BF16), pltpu.VMEM((blk, blk), BF16)],
        compiler_params=_cparams(2),
        name="sb_attn",
    )(qt4, k5, vt5, g.reshape(1, D_SB))


def _conv_kernel(u_ref, uh_ref, cw_ref, cb_ref, lg_ref, lb_ref, o_ref, pext_ref, pinv_ref, cwb_ref, gext_ref,
                 yp_ref):
    step = pl.program_id(1)
    ts = TS_CONV
    ext = CONV_EXT_ROWS

    @pl.when(jnp.logical_and(pl.program_id(0) == 0, step == 0))
    def _():
        e = lax.broadcasted_iota(I32, (ext, ts + CONV_HALO), 0)
        c = lax.broadcasted_iota(I32, (ext, ts + CONV_HALO), 1)
        src_row = (e & (SUBLANES - 1)) * CONV_L + (e >> 3)
        pext_ref[...] = jnp.where(c == src_row, 1.0, 0.0).astype(BF16)
        t = lax.broadcasted_iota(I32, (ts, ts), 0)
        p = lax.broadcasted_iota(I32, (ts, ts), 1)
        pinv_ref[...] = jnp.where(p == (t & (CONV_L - 1)) * SUBLANES + t // CONV_L, 1.0, 0.0).astype(BF16)
        for w in range(CONV_WIDTH):
            cwb_ref[w] = jnp.broadcast_to(cw_ref[w:w + 1, :], (SUBLANES, D_CONV))

    def glu(u):
        a = u[:, :D_CONV].astype(F32)
        gate = u[:, D_CONV:].astype(F32)
        return a * jax.nn.sigmoid(gate)

    gh = jnp.where(step > 0, glu(uh_ref[0]), 0.0)
    g = jnp.concatenate([gh, glu(u_ref[0])], axis=0).astype(BF16)
    gext_ref[...] = jnp.dot(pext_ref[...], g, preferred_element_type=F32)
    first = CONV_HALO - (CONV_WIDTH - 1)
    rows = CONV_CHUNK

    def chunk(c, _):
        r0 = pl.multiple_of(c * rows, rows)
        acc = jnp.zeros((rows // SUBLANES, SUBLANES, D_CONV), F32)
        for w in range(CONV_WIDTH):
            start = pl.multiple_of(r0 + (first + w) * SUBLANES, SUBLANES)
            gw = gext_ref[pl.ds(start, rows), :].reshape(rows // SUBLANES, SUBLANES, D_CONV)
            acc = acc + gw * cwb_ref[w]
        yp_ref[pl.ds(r0, rows), :] = acc.reshape(rows, D_CONV)
        return 0

    lax.fori_loop(0, ts // rows, chunk, 0)
    y = yp_ref[...] + cb_ref[...]
    mu = jnp.mean(y, axis=-1, keepdims=True)
    d = y - mu
    var = jnp.mean(d * d, axis=-1, keepdims=True)
    yn = d * lax.rsqrt(var + EPS) * lg_ref[...] + lb_ref[...]
    act = (yn * jax.nn.sigmoid(yn)).astype(BF16)
    o_ref[0] = jnp.dot(pinv_ref[...], act, preferred_element_type=F32).astype(BF16)


def _conv_branch(u3, conv_w, conv_b, ln_g, ln_b):
    B, S, _ = u3.shape
    ts = TS_CONV
    halo_per_step = ts // CONV_HALO
    vec = lambda a: a.reshape(1, D_CONV)
    cw = jnp.concatenate([conv_w, jnp.zeros((CONV_HALO - CONV_WIDTH, D_CONV), F32)], axis=0)
    const = lambda rows: pl.BlockSpec((rows, D_CONV), lambda b, i: (0, 0))
    return pl.pallas_call(
        _conv_kernel,
        grid=(B, S // ts),
        in_specs=[pl.BlockSpec((1, ts, 2 * D_CONV), lambda b, i: (b, i, 0)),
                  pl.BlockSpec((1, CONV_HALO, 2 * D_CONV),
                               lambda b, i: (b, jnp.maximum(i * halo_per_step - 1, 0), 0)),
                  const(CONV_HALO), const(1), const(1), const(1)],
        out_specs=pl.BlockSpec((1, ts, D_CONV), lambda b, i: (b, i, 0)),
        out_shape=jax.ShapeDtypeStruct((B, S, D_CONV), BF16),
        scratch_shapes=[pltpu.VMEM((CONV_EXT_ROWS, ts + CONV_HALO), BF16),
                        pltpu.VMEM((ts, ts), BF16),
                        pltpu.VMEM((CONV_HALO, SUBLANES, D_CONV), F32),
                        pltpu.VMEM((CONV_EXT_ROWS, D_CONV), F32),
                        pltpu.VMEM((ts, D_CONV), F32)],
        compiler_params=_cparams(2),
        name="conv",
    )(u3, u3, cw, vec(conv_b), vec(ln_g), vec(ln_b))


def _memkv_kernel(m_ref, g_ref, w_ref, kt_ref, v_ref):
    mn = _rms(m_ref[0], g_ref[...]).astype(BF16)
    kv = jnp.dot(mn, w_ref[...], preferred_element_type=F32)
    kt_ref[0] = kv[:, :D_MODEL].T.astype(BF16)
    v_ref[0] = kv[:, D_MODEL:].astype(BF16)


def _memkv(mem, g, w):
    B, M, _ = mem.shape
    return pl.pallas_call(
        _memkv_kernel,
        grid=(B,),
        in_specs=[pl.BlockSpec((1, M, D_MODEL), lambda b: (b, 0, 0)),
                  pl.BlockSpec((1, D_MODEL), lambda b: (0, 0)),
                  pl.BlockSpec((D_MODEL, 2 * D_MODEL), lambda b: (0, 0))],
        out_specs=[pl.BlockSpec((1, D_MODEL, M), lambda b: (b, 0, 0)),
                   pl.BlockSpec((1, M, D_MODEL), lambda b: (b, 0, 0))],
        out_shape=[jax.ShapeDtypeStruct((B, D_MODEL, M), BF16),
                   jax.ShapeDtypeStruct((B, M, D_MODEL), BF16)],
        compiler_params=_cparams(1),
        name="memkv",
    )(mem, g.reshape(1, D_MODEL), w)


def _post_kernel(*refs):
    cnt_ref = refs[-1]

    @pl.when(jnp.logical_and(pl.program_id(0) == 0, pl.program_id(1) == 0))
    def _():
        cnt_ref[...] = jnp.zeros_like(cnt_ref)

    groups = [_post_rows(part, *refs) for part in range(POST_SPLIT)]
    live = True
    step = 0
    while live:
        live = False
        for part, group in enumerate(groups):
            if step >= part and next(group, None) is not None:
                live = True
        step += 1


def _post_rows(part, x_ref, sb_ref, cv_ref, wo_ref, gx_ref, wq_ref, kt_ref, vx_ref, wxo_ref, gf_ref,
               wr_ref, br_ref, x2_ref, hp_ref, ri_ref, cnt_ref):
    tm = TM_POST // POST_SPLIT
    rows = slice(part * tm, (part + 1) * tm)

    x1 = (x_ref[0, rows, :]
          + jnp.dot(sb_ref[0, rows, :], wo_ref[:D_SB, :], preferred_element_type=F32)
          + jnp.dot(cv_ref[0, rows, :], wo_ref[D_SB:, :], preferred_element_type=F32))
    yield 1

    hq = _rms(x1, gx_ref[...]).astype(BF16)
    yield 2
    q = (jnp.dot(hq, wq_ref[...], preferred_element_type=F32) * (XHEAD_DIM ** -0.5)).astype(BF16)
    yield 3
    outs = []
    for h in range(N_XHEADS):
        hs = slice(h * XHEAD_DIM, (h + 1) * XHEAD_DIM)
        s = jnp.dot(q[:, hs], kt_ref[0, hs, :], preferred_element_type=F32)
        s = s - jnp.max(s, axis=-1, keepdims=True)
        p = jnp.exp(s)
        p = p / jnp.sum(p, axis=-1, keepdims=True)
        outs.append(jnp.dot(p.astype(BF16), vx_ref[0, :, hs], preferred_element_type=F32))
    o = jnp.concatenate(outs, axis=1).astype(BF16)
    yield 4
    x2 = x1 + jnp.dot(o, wxo_ref[...], preferred_element_type=F32)
    x2_ref[0, rows, :] = x2
    yield 5

    hn = _rms(x2, gf_ref[...])
    _store_token_rows(hp_ref, part * tm, _pack_rows(hn))
    yield 6
    logits = jnp.dot(hn.astype(BF16), wr_ref[...], preferred_element_type=F32) + br_ref[...]
    yield 7
    lane = lax.broadcasted_iota(I32, (tm, LANES), 1)
    big = jnp.int32(LANES)
    ninf = jnp.float32(-jnp.inf)

    def first_argmax(vals):
        m = jnp.max(vals, axis=-1, keepdims=True)
        idx = jnp.min(jnp.where(vals == m, lane, big), axis=-1, keepdims=True)
        return m, idx

    is_group = lane < N_GROUPS
    gl = jnp.where(is_group, logits, ninf)
    gmax, gidx = first_argmax(gl)
    gsum = jnp.sum(jnp.where(is_group, jnp.exp(logits - gmax), 0.0), axis=-1, keepdims=True)
    g_w = 1.0 / gsum
    lo = RLANE_E + gidx * EXPERTS_PER_GROUP
    in_group = jnp.logical_and(lane >= lo, lane < lo + EXPERTS_PER_GROUP)
    el = jnp.where(in_group, logits, ninf)
    v1, i1 = first_argmax(el)
    el2 = jnp.where(lane == i1, ninf, el)
    v2, i2 = first_argmax(el2)
    t = jnp.exp(v2 - v1)
    w1 = g_w / (1.0 + t)
    w2 = g_w * t / (1.0 + t)
    yield 8

    hit1 = lane == i1
    hit2 = lane == i2
    oh = jnp.where(jnp.logical_or(hit1, hit2), 1.0, 0.0)
    rr = lax.broadcasted_iota(I32, (tm, tm), 0)
    cc = lax.broadcasted_iota(I32, (tm, tm), 1)
    ltri = jnp.where(cc < rr, 1.0, 0.0).astype(BF16)
    before = jnp.dot(ltri, oh.astype(BF16), preferred_element_type=F32) + cnt_ref[0:1, :]
    rank1 = jnp.sum(jnp.where(hit1, before, 0.0), axis=-1, keepdims=True)
    rank2 = jnp.sum(jnp.where(hit2, before, 0.0), axis=-1, keepdims=True)
    cnt_ref[...] = cnt_ref[...] + jnp.sum(oh, axis=0, keepdims=True)

    e1 = (i1 - RLANE_E).astype(F32)
    e2 = (i2 - RLANE_E).astype(F32)
    cols = (e1, e2, w1, w2, rank1, rank2)
    info = jnp.zeros((tm, LANES), F32)
    for c, val in enumerate(cols):
        info = jnp.where(lane == c, val, info)
    ri_ref[rows, :] = info


def _post(x, sb, cv, w_out, gx, w_xq, kt, vx, w_xo, gf, w_router, b_router):
    B, S, _ = x.shape
    T = B * S
    tm = TM_POST
    spt = S // tm
    tok = lambda width: pl.BlockSpec((1, tm, width), lambda b, i: (b, i, 0))
    flat = lambda width: pl.BlockSpec((tm, width), lambda b, i: (b * spt + i, 0))
    const = lambda r, c: pl.BlockSpec((r, c), lambda b, i: (0, 0))
    M = vx.shape[1]
    return pl.pallas_call(
        _post_kernel,
        grid=(B, spt),
        in_specs=[tok(D_MODEL), tok(D_SB), tok(D_CONV),
                  const(D_MODEL, D_MODEL), const(1, D_MODEL), const(D_MODEL, D_MODEL),
                  pl.BlockSpec((1, D_MODEL, M), lambda b, i: (b, 0, 0)),
                  pl.BlockSpec((1, M, D_MODEL), lambda b, i: (b, 0, 0)),
                  const(D_MODEL, D_MODEL), const(1, D_MODEL),
                  const(D_MODEL, LANES), const(1, LANES)],
        out_specs=[tok(D_MODEL), pl.BlockSpec((tm * ROW_CHUNKS, LANES), lambda b, i: (b * spt + i, 0)), flat(LANES),
                   pl.BlockSpec((SUBLANES, LANES), lambda b, i: (0, 0))],
        out_shape=[jax.ShapeDtypeStruct((B, S, D_MODEL), F32),
                   jax.ShapeDtypeStruct((T * ROW_CHUNKS, LANES), U32),
                   jax.ShapeDtypeStruct((T, LANES), F32),
                   jax.ShapeDtypeStruct((SUBLANES, LANES), F32)],
        compiler_params=_cparams(2),
        name="post",
    )(x, sb, cv, w_out, gx.reshape(1, D_MODEL), w_xq, kt, vx, w_xo, gf.reshape(1, D_MODEL),
      w_router, b_router)


def _token_rows(ref, token):
    return ref.at[pl.ds(pl.multiple_of(token * ROW_CHUNKS, ROW_CHUNKS), ROW_CHUNKS)]


def _row_copy_wait(src_rows, dst_rows, sem, n):
    pltpu.make_async_copy(src_rows.at[pl.ds(0, n * ROW_CHUNKS)], dst_rows.at[pl.ds(0, n * ROW_CHUNKS)], sem).wait()


def _dispatch_kernel(pend_ref, dest_ref, hp_ref, xd_ref, idx_ref, zero_ref, sem_idx, sem_rows, sem_zero):
    tm = TM_ROWS
    i = pl.program_id(0)
    n = pl.num_programs(0)
    slot = i % 2

    def idx_copy(step, s):
        return pltpu.make_async_copy(dest_ref.at[step], idx_ref.at[s], sem_idx.at[s])

    def zero_copy(e):
        start = pl.multiple_of((pend_ref[e] - TM_EXP) * ROW_CHUNKS, TM_EXP * ROW_CHUNKS)
        return pltpu.make_async_copy(zero_ref, xd_ref.at[pl.ds(start, TM_EXP * ROW_CHUNKS)], sem_zero)

    def nonempty(e):
        return pend_ref[e] > (pend_ref[e - 1] if e else 0)

    @pl.when(i == 0)
    def _():
        idx_copy(0, 0).start()
        zero_ref[...] = jnp.zeros_like(zero_ref)
        for e in range(N_EXPERTS):
            pl.when(nonempty(e))(lambda e=e: zero_copy(e).start())
        for e in range(N_EXPERTS):
            pl.when(nonempty(e))(lambda e=e: zero_copy(e).wait())

        def zero_unused_tile(j, _):
            start = pl.multiple_of(j * (TM_EXP * ROW_CHUNKS), TM_EXP * ROW_CHUNKS)
            cp = pltpu.make_async_copy(zero_ref, xd_ref.at[pl.ds(start, TM_EXP * ROW_CHUNKS)], sem_zero)
            cp.start()
            cp.wait()
            return 0

        n_tiles = xd_ref.shape[0] // (TM_EXP * ROW_CHUNKS)
        lax.fori_loop(pend_ref[N_EXPERTS - 1] // TM_EXP, n_tiles, zero_unused_tile, 0)

    @pl.when(i > 0)
    def _():
        for k in range(2):
            _row_copy_wait(hp_ref, xd_ref, sem_rows, tm)

    idx_copy(i, slot).wait()

    @pl.when(i + 1 < n)
    def _():
        idx_copy(i + 1, 1 - slot).start()

    base = i * tm
    for r0 in range(0, tm, ISSUE_BATCH):
        batch = [(r, idx_ref[slot, k * tm + r]) for r in range(r0, r0 + ISSUE_BATCH) for k in range(2)]
        for r, d in batch:
            pltpu.make_async_copy(_token_rows(hp_ref, base + r), _token_rows(xd_ref, d), sem_rows).start()

    @pl.when(i == n - 1)
    def _():
        for k in range(2):
            _row_copy_wait(hp_ref, xd_ref, sem_rows, tm)


def _dispatch(pend, dest, hp, n_rows):
    T = hp.shape[0] // ROW_CHUNKS
    tm = TM_ROWS
    grid_spec = pltpu.PrefetchScalarGridSpec(
        num_scalar_prefetch=1,
        grid=(T // tm,),
        in_specs=[pl.BlockSpec(memory_space=pl.ANY), pl.BlockSpec(memory_space=pl.ANY)],
        out_specs=pl.BlockSpec(memory_space=pl.ANY),
        scratch_shapes=[pltpu.SMEM((2, 2 * tm), I32), pltpu.VMEM((TM_EXP * ROW_CHUNKS, LANES), U32),
                        pltpu.SemaphoreType.DMA((2,)), pltpu.SemaphoreType.DMA, pltpu.SemaphoreType.DMA],
    )
    return pl.pallas_call(
        _dispatch_kernel,
        grid_spec=grid_spec,
        out_shape=jax.ShapeDtypeStruct((n_rows * ROW_CHUNKS, LANES), U32),
        compiler_params=_cparams(1),
        name="dispatch",
    )(pend, dest, hp)


def _experts_kernel(te_ref, nv_ref, xd_ref, wgu_ref, wd_ref, yd_ref):
    del te_ref
    sub_rows = TM_EXP // EXP_SPLIT

    def rows_pass(part):
        lo, hi = _unpack_rows(_load_token_rows(xd_ref, part * sub_rows, sub_rows))
        lo, hi = lo.astype(BF16), hi.astype(BF16)
        yield 1
        gu = (jnp.dot(lo, wgu_ref[0, :HALF, :], preferred_element_type=F32)
              + jnp.dot(hi, wgu_ref[0, HALF:, :], preferred_element_type=F32))
        yield 2
        gate = gu[:, :D_EXPERT]
        up = gu[:, D_EXPERT:]
        hmid = (gate * jax.nn.sigmoid(gate) * up).astype(BF16)
        yield 3
        y = jnp.dot(hmid, wd_ref[0], preferred_element_type=F32)
        yield 4
        _store_token_rows(yd_ref, part * sub_rows, _pack_rows(y))

    @pl.when(pl.program_id(0) >= nv_ref[0])
    def _():
        yd_ref[...] = jnp.zeros_like(yd_ref)

    @pl.when(pl.program_id(0) < nv_ref[0])
    def _():
        groups = [rows_pass(part) for part in range(EXP_SPLIT)]
        live = True
        step = 0
        while live:
            live = False
            for part, group in enumerate(groups):
                if step >= part and next(group, None) is not None:
                    live = True
            step += 1


def _experts(tile_expert, n_valid, xd, wgu, wd):
    n_tiles = tile_expert.shape[0]
    tm = TM_EXP
    tile = lambda i, te, nv: (jnp.minimum(i, nv[0] - 1), 0)
    weight = lambda i, te, nv: (te[jnp.minimum(i, nv[0] - 1)], 0, 0)
    grid_spec = pltpu.PrefetchScalarGridSpec(
        num_scalar_prefetch=2,
        grid=(n_tiles,),
        in_specs=[pl.BlockSpec((tm * ROW_CHUNKS, LANES), tile),
                  pl.BlockSpec((1, D_MODEL, 2 * D_EXPERT), weight),
                  pl.BlockSpec((1, D_EXPERT, D_MODEL), weight)],
        out_specs=pl.BlockSpec((tm * ROW_CHUNKS, LANES), lambda i, te, nv: (i, 0)),
    )
    return pl.pallas_call(
        _experts_kernel,
        grid_spec=grid_spec,
        out_shape=jax.ShapeDtypeStruct((n_tiles * tm * ROW_CHUNKS, LANES), U32),
        compiler_params=_cparams(1),
        name="experts",
    )(tile_expert, n_valid, xd, wgu, wd)


def _combine_kernel(dest_ref, yd_ref, x2_ref, ri_ref, g_ref, o_ref, idx_ref, rows_ref, sem_idx, sem_rows):
    tm = TM_ROWS
    i = pl.program_id(0)
    n = pl.num_programs(0)
    slot = i % 2

    def fetch(step, s):
        cp = pltpu.make_async_copy(dest_ref.at[step], idx_ref.at[s], sem_idx)
        cp.start()
        cp.wait()
        for r0 in range(0, tm, ISSUE_BATCH):
            batch = [(r, k, idx_ref[s, k * tm + r]) for r in range(r0, r0 + ISSUE_BATCH) for k in range(2)]
            for r, k, d in batch:
                pltpu.make_async_copy(_token_rows(yd_ref, d), _token_rows(rows_ref.at[s, k], r),
                                      sem_rows.at[s]).start()

    @pl.when(i == 0)
    def _():
        fetch(0, 0)

    @pl.when(i + 1 < n)
    def _():
        fetch(i + 1, 1 - slot)

    for k in range(2):
        _row_copy_wait(yd_ref, rows_ref.at[slot, k], sem_rows.at[slot], tm)

    ri = ri_ref[...]
    x2 = x2_ref[...]
    y_lo = jnp.zeros((tm, HALF), F32)
    y_hi = jnp.zeros((tm, HALF), F32)
    for k in range(2):
        lo, hi = _unpack_rows(_load_token_rows(rows_ref.at[slot, k], 0, tm))
        wk = ri[:, 2 + k:3 + k]
        y_lo = y_lo + wk * lo
        y_hi = y_hi + wk * hi
    x3 = x2 + jnp.concatenate([y_lo, y_hi], axis=1)
    o_ref[...] = _rms(x3, g_ref[...])


def _combine(dest, yd, x2, rinfo, g):
    T = x2.shape[0]
    tm = TM_ROWS
    return pl.pallas_call(
        _combine_kernel,
        grid=(T // tm,),
        in_specs=[pl.BlockSpec(memory_space=pl.ANY),
                  pl.BlockSpec(memory_space=pl.ANY),
                  pl.BlockSpec((tm, D_MODEL), lambda i: (i, 0)),
                  pl.BlockSpec((tm, LANES), lambda i: (i, 0)),
                  pl.BlockSpec((1, D_MODEL), lambda i: (0, 0))],
        out_specs=pl.BlockSpec((tm, D_MODEL), lambda i: (i, 0)),
        out_shape=jax.ShapeDtypeStruct((T, D_MODEL), F32),
        scratch_shapes=[pltpu.SMEM((2, 2 * tm), I32), pltpu.VMEM((2, 2, tm * ROW_CHUNKS, LANES), U32),
                        pltpu.SemaphoreType.DMA, pltpu.SemaphoreType.DMA((2,))],
        compiler_params=_cparams(1),
        name="combine",
    )(dest, yd, x2, rinfo, g.reshape(1, D_MODEL))


def _routing_tables(rinfo, counts_row, T):
    tm = TM_EXP
    counts = counts_row[RLANE_E:RLANE_E + N_EXPERTS].astype(I32)
    padded = (counts + tm - 1) // tm * tm
    pend = jnp.cumsum(padded)
    pstart = pend - padded
    e = rinfo[:, 0:2].astype(I32)
    rank = rinfo[:, 4:6].astype(I32)
    ids = jnp.arange(N_EXPERTS, dtype=I32)
    dest = rank + jnp.sum(jnp.where(e[..., None] == ids, pstart, 0), axis=-1)
    n_tiles = (2 * T) // tm + N_EXPERTS
    tile_start = jnp.arange(n_tiles, dtype=I32) * tm
    tile_expert = jnp.minimum(jnp.sum((tile_start[:, None] >= pend[None, :]).astype(I32), axis=1),
                              N_EXPERTS - 1)
    n_valid = (pend[-1:] // tm).astype(I32)
    ntt = T // TM_ROWS
    dest_tiles = dest.reshape(ntt, TM_ROWS, 2).transpose(0, 2, 1).reshape(ntt, 2 * TM_ROWS)
    return dest_tiles, tile_expert, n_valid, pend.astype(I32), n_tiles


def kernel(x, mem, ln_mix_g, w_in, sb_out_g, conv_w, conv_b, conv_ln_g, conv_ln_b, w_out, ln_mem_x_g,
           ln_mem_g, w_xq, w_xkv, w_xo, ln_ffn_g, w_group, b_group, w_er, b_er, w_gate, w_up, w_down,
           ln_final_g):
    B, S, D = x.shape
    T = B * S
    assert ln_mix_g.shape[0] == 1, "single-layer pipeline: the output norm is fused into the combine kernel"
    assert D == D_MODEL and S % TS_CONV == 0 and S % TM_PROJ == 0 and T % TM_ROWS == 0
    l = 0
    qt4, k5, vt5, u = _inproj(x, ln_mix_g[l], w_in[l].astype(BF16))
    sb = _sb_attention(qt4, k5, vt5, sb_out_g[l])
    cv = _conv_branch(u, conv_w[l], conv_b[l], conv_ln_g[l], conv_ln_b[l])
    kt, vx = _memkv(mem, ln_mem_g[l], w_xkv[l].astype(BF16))

    pad = LANES - RLANE_E - N_EXPERTS
    w_router = jnp.concatenate([w_group[l], w_er[l].reshape(D, N_EXPERTS), jnp.zeros((D, pad), F32)], axis=1)
    b_router = jnp.concatenate([b_group[l], b_er[l].reshape(N_EXPERTS), jnp.zeros((pad,), F32)])
    x2, hp, rinfo, cnt = _post(x, sb, cv, w_out[l].astype(BF16), ln_mem_x_g[l], w_xq[l].astype(BF16),
                               kt, vx, w_xo[l].astype(BF16), ln_ffn_g[l],
                               w_router.astype(BF16), b_router.reshape(1, LANES))

    dest_tiles, tile_expert, n_valid, pend, n_tiles = _routing_tables(rinfo, cnt[0], T)
    xd = _dispatch(pend, dest_tiles, hp, n_tiles * TM_EXP)
    wgu = jnp.concatenate([w_gate[l], w_up[l]], axis=-1).astype(BF16)
    yd = _experts(tile_expert, n_valid, xd, wgu, w_down[l].astype(BF16))
    return _combine(dest_tiles, yd, x2.reshape(T, D), rinfo, ln_final_g).reshape(B, S, D)
```

```python
import functools

import jax
import jax.numpy as jnp
from jax import lax
from jax.experimental import pallas as pl
from jax.experimental.pallas import tpu as pltpu

F32 = jnp.float32
BF16 = jnp.bfloat16
I32 = jnp.int32
U32 = jnp.uint32

EPS = 1e-6

D_MODEL = 1024
N_SB_HEADS = 8
SB_HEAD_DIM = 64
D_SB = N_SB_HEADS * SB_HEAD_DIM
D_CONV = D_MODEL - D_SB
CONV_WIDTH = 31
N_XHEADS = 4
XHEAD_DIM = D_MODEL // N_XHEADS
N_GROUPS = 4
EXPERTS_PER_GROUP = 8
N_EXPERTS = N_GROUPS * EXPERTS_PER_GROUP
D_EXPERT = 512

LANES = 128
SUBLANES = 8
VMEM_LIMIT = 56 * 1024 * 1024

TM_PROJ = 512
SB_BLK = 256
SB_GROUP = SB_BLK // SUBLANES
TS_CONV = 512
CONV_HALO = 32
CONV_CHUNK = 32
CONV_L = TS_CONV // SUBLANES
CONV_EXT_ROWS = (CONV_L + CONV_HALO) * SUBLANES
TM_POST = 512
POST_SPLIT = 2
TM_ROWS = 512
TM_EXP = 512
EXP_SPLIT = 2
ISSUE_BATCH = 8
HALF = D_MODEL // 2
ROW_CHUNKS = HALF // LANES
RLANE_E = 4
NEG_BIG = -1e30
LOG2E = 1.4426950408889634
LOGIT_CAP = 64.0


def _cparams(n_axes):
    return pltpu.CompilerParams(dimension_semantics=("arbitrary",) * n_axes,
                                vmem_limit_bytes=VMEM_LIMIT)


def _rms(x, g):
    ms = jnp.mean(x * x, axis=-1, keepdims=True)
    return x * lax.rsqrt(ms + EPS) * g


def _pack_rows(y):
    lo = pltpu.bitcast(y[:, :HALF].astype(BF16).astype(F32), U32)
    hi = pltpu.bitcast(y[:, HALF:].astype(BF16).astype(F32), U32)
    return (lo >> 16) | (hi & jnp.uint32(0xFFFF0000))


def _unpack_rows(w):
    lo = pltpu.bitcast(w << 16, F32)
    hi = pltpu.bitcast(w & jnp.uint32(0xFFFF0000), F32)
    return lo, hi


def _store_token_rows(ref, first, words):
    rows = words.shape[0]
    for c in range(ROW_CHUNKS):
        ref[pl.ds(first * ROW_CHUNKS + c, rows, stride=ROW_CHUNKS), :] = words[:, c * LANES:(c + 1) * LANES]


def _load_token_rows(ref, first, rows):
    return jnp.concatenate([ref[pl.ds(first * ROW_CHUNKS + c, rows, stride=ROW_CHUNKS), :]
                            for c in range(ROW_CHUNKS)], axis=1)


def _inproj_kernel(x_ref, g_ref, w_ref, qt_ref, k_ref, vt_ref, u_ref, perm_ref):
    blk = SB_BLK
    npair = N_SB_HEADS // 2

    @pl.when(jnp.logical_and(pl.program_id(0) == 0, pl.program_id(1) == 0))
    def _():
        p = lax.broadcasted_iota(I32, (blk, blk), 0)
        c = lax.broadcasted_iota(I32, (blk, blk), 1)
        perm_ref[...] = jnp.where(c == (p & (SUBLANES - 1)) * SB_GROUP + (p >> 3), 1.0, 0.0).astype(BF16)

    h = _rms(x_ref[0], g_ref[...]).astype(BF16)
    p = jnp.dot(h, w_ref[...], preferred_element_type=F32)
    u_ref[0] = p[:, 3 * D_SB:].astype(BF16)
    q = p[:, :D_SB] * (SB_HEAD_DIM ** -0.5 * LOG2E)
    for j in range(npair):
        qt_ref[0, j] = q[:, j * LANES:(j + 1) * LANES].T.astype(BF16)
    for nb in range(TM_PROJ // blk):
        rows = slice(nb * blk, (nb + 1) * blk)
        kv = p[rows, D_SB:3 * D_SB].astype(BF16)
        kvp = jnp.dot(perm_ref[...], kv, preferred_element_type=F32)
        for j in range(npair):
            k_ref[0, j, nb] = kvp[:, j * LANES:(j + 1) * LANES].astype(BF16)
            vt = kvp[:, D_SB + j * LANES:D_SB + (j + 1) * LANES].T.astype(BF16)
            vt_ref[0, 2 * j, nb] = vt[:SB_HEAD_DIM]
            vt_ref[0, 2 * j + 1, nb] = vt[SB_HEAD_DIM:]


def _inproj(x, g, w):
    B, S, _ = x.shape
    n_out = w.shape[1]
    tm = TM_PROJ
    blk = SB_BLK
    nb = tm // blk
    npair = N_SB_HEADS // 2
    return pl.pallas_call(
        _inproj_kernel,
        grid=(B, S // tm),
        in_specs=[pl.BlockSpec((1, tm, D_MODEL), lambda b, i: (b, i, 0)),
                  pl.BlockSpec((1, D_MODEL), lambda b, i: (0, 0)),
                  pl.BlockSpec((D_MODEL, n_out), lambda b, i: (0, 0))],
        out_specs=[pl.BlockSpec((1, npair, LANES, tm), lambda b, i: (b, 0, 0, i)),
                   pl.BlockSpec((1, npair, nb, blk, LANES), lambda b, i: (b, 0, i, 0, 0)),
                   pl.BlockSpec((1, N_SB_HEADS, nb, SB_HEAD_DIM, blk), lambda b, i: (b, 0, i, 0, 0)),
                   pl.BlockSpec((1, tm, 2 * D_CONV), lambda b, i: (b, i, 0))],
        out_shape=[jax.ShapeDtypeStruct((B, npair, LANES, S), BF16),
                   jax.ShapeDtypeStruct((B, npair, S // blk, blk, LANES), BF16),
                   jax.ShapeDtypeStruct((B, N_SB_HEADS, S // blk, SB_HEAD_DIM, blk), BF16),
                   jax.ShapeDtypeStruct((B, S, 2 * D_CONV), BF16)],
        scratch_shapes=[pltpu.VMEM((blk, blk), BF16)],
        compiler_params=_cparams(2),
        name="inproj",
    )(x, g.reshape(1, D_MODEL), w)


def _sb_kernel(qt_ref, k_ref, vt_ref, g_ref, o_ref, ot_ref, carry_ref, qmt_ref, bias_ref,
               z0, z1, z2, e0, e1, e2, w0, w1, w2, at0, at1, at2):
    qi = pl.program_id(1)
    blk = SB_BLK
    n_items = N_SB_HEADS * (qi + 1)

    @pl.when(jnp.logical_and(pl.program_id(0) == 0, qi == 0))
    def _():
        rowp = lax.broadcasted_iota(I32, (blk, blk), 0)
        colc = lax.broadcasted_iota(I32, (blk, blk), 1)
        key_local = (rowp & (SUBLANES - 1)) * SB_GROUP + (rowp >> 3)
        bias_ref[0] = jnp.zeros((blk, blk), F32)
        bias_ref[1] = jnp.where(key_local < colc, 0.0, NEG_BIG)
        bias_ref[2] = jnp.full((blk, blk), NEG_BIG, F32)

    ot_ref[...] = jnp.zeros_like(ot_ref)
    carry_ref[...] = jnp.ones_like(carry_ref)
    at1[...] = jnp.zeros_like(at1)
    at2[...] = jnp.zeros_like(at2)
    e2[...] = jnp.zeros_like(e2)
    w2[...] = jnp.zeros_like(w2)
    rowi = lax.broadcasted_iota(I32, (LANES, blk), 0)
    for j in range(N_SB_HEADS // 2):
        qs = qt_ref[0, j]
        zero = jnp.zeros_like(qs)
        qmt_ref[2 * j] = jnp.where(rowi < SB_HEAD_DIM, qs, zero)
        qmt_ref[2 * j + 1] = jnp.where(rowi >= SB_HEAD_DIM, qs, zero)
    sub = lax.broadcasted_iota(I32, (SUBLANES, blk), 0)

    def item(m):
        mc = jnp.minimum(m, n_items - 1)
        return qi - (mc >> 3), mc & (N_SB_HEADS - 1)

    def z_phase(m, zw):
        kb, head = item(m)
        zw[...] = jnp.dot(k_ref[0, head >> 1, kb], qmt_ref[head], preferred_element_type=F32)

    def e1_phase(m, zr, er, wr):
        kb, head = item(m)
        bidx = jnp.where(m >= n_items, 2, jnp.where(kb == qi, 1, 0))
        pex = jnp.ones((SUBLANES, blk), F32)
        for i in reversed(range(SB_GROUP)):
            rows = slice(SUBLANES * i, SUBLANES * (i + 1))
            z = zr[rows, :] + bias_ref[bidx, rows, :]
            t = jnp.exp2(jnp.minimum(z, LOGIT_CAP))
            keep = 1.0 / (1.0 + t)
            er[rows, :] = (t * keep) * pex
            pex = pex * keep
        incl = pex
        for s in (1, 2, 4):
            shifted = pltpu.roll(incl, SUBLANES - s, 0)
            incl = incl * jnp.where(sub + s < SUBLANES, shifted, 1.0)
        later = jnp.where(sub + 1 < SUBLANES, pltpu.roll(incl, SUBLANES - 1, 0), 1.0)
        carry = carry_ref[head]
        w = carry * later
        carry_ref[head] = carry * jnp.broadcast_to(incl[0:1], (SUBLANES, blk))
        wr[...] = jnp.concatenate([w, w], axis=0)

    def e2_phase(er, wr, atw):
        scale = wr[...]
        for i in range(0, SB_GROUP, 2):
            rows = slice(SUBLANES * i, SUBLANES * (i + 2))
            atw[rows, :] = (er[rows, :] * scale).astype(BF16)

    def av_phase(m, atr):
        kb, head = item(jnp.maximum(m - 2, 0))
        ot_ref[head] = ot_ref[head] + jnp.dot(vt_ref[0, head, kb], atr[...], preferred_element_type=F32)

    zs, es, ws, ats = (z0, z1, z2), (e0, e1, e2), (w0, w1, w2), (at0, at1, at2)
    z_phase(0, z0)
    z_phase(1, z1)

    def body(t, _):
        for u in range(3):
            m = 3 * t + u
            z_phase(m + 2, zs[(u + 2) % 3])
            e1_phase(m, zs[u], es[u], ws[u])
            e2_phase(es[(u + 2) % 3], ws[(u + 2) % 3], ats[(u + 2) % 3])
            av_phase(m, ats[(u + 1) % 3])
        return 0

    lax.fori_loop(0, (n_items + 4) // 3, body, 0)
    parts = [ot_ref[2 * j:2 * j + 2].reshape(LANES, blk).T for j in range(N_SB_HEADS // 2)]
    o_ref[0] = _rms(jnp.concatenate(parts, axis=1), g_ref[...]).astype(BF16)


def _sb_attention(qt4, k5, vt5, g):
    B, npair, _, S = qt4.shape
    nkb = S // SB_BLK
    blk = SB_BLK
    return pl.pallas_call(
        _sb_kernel,
        grid=(B, nkb),
        in_specs=[pl.BlockSpec((1, npair, LANES, blk), lambda b, i: (b, 0, 0, i)),
                  pl.BlockSpec((1, npair, nkb, blk, LANES), lambda b, i: (b, 0, 0, 0, 0)),
                  pl.BlockSpec((1, N_SB_HEADS, nkb, SB_HEAD_DIM, blk), lambda b, i: (b, 0, 0, 0, 0)),
                  pl.BlockSpec((1, D_SB), lambda b, i: (0, 0))],
        out_specs=pl.BlockSpec((1, blk, D_SB), lambda b, i: (b, i, 0)),
        out_shape=jax.ShapeDtypeStruct((B, S, D_SB), BF16),
        scratch_shapes=[pltpu.VMEM((N_SB_HEADS, SB_HEAD_DIM, blk), F32),
                        pltpu.VMEM((N_SB_HEADS, SUBLANES, blk), F32),
                        pltpu.VMEM((N_SB_HEADS, LANES, blk), BF16),
                        pltpu.VMEM((3, blk, blk), F32),
                        ] + [pltpu.VMEM((blk, blk), F32)] * 6
                        + [pltpu.VMEM((2 * SUBLANES, blk), F32)] * 3
                        + [pltpu.VMEM((blk, blk), BF16)] * 3,
        compiler_params=_cparams(2),
        name="sb_attn",
    )(qt4, k5, vt5, g.reshape(1, D_SB))


def _conv_kernel(u_ref, uh_ref, cw_ref, cb_ref, lg_ref, lb_ref, o_ref, pext_ref, pinv_ref, cwb_ref, gext_ref,
                 yp_ref):
    step = pl.program_id(1)
    ts = TS_CONV
    ext = CONV_EXT_ROWS

    @pl.when(jnp.logical_and(pl.program_id(0) == 0, step == 0))
    def _():
        e = lax.broadcasted_iota(I32, (ext, ts + CONV_HALO), 0)
        c = lax.broadcasted_iota(I32, (ext, ts + CONV_HALO), 1)
        src_row = (e & (SUBLANES - 1)) * CONV_L + (e >> 3)
        pext_ref[...] = jnp.where(c == src_row, 1.0, 0.0).astype(BF16)
        t = lax.broadcasted_iota(I32, (ts, ts), 0)
        p = lax.broadcasted_iota(I32, (ts, ts), 1)
        pinv_ref[...] = jnp.where(p == (t & (CONV_L - 1)) * SUBLANES + t // CONV_L, 1.0, 0.0).astype(BF16)
        for w in range(CONV_WIDTH):
            cwb_ref[w] = jnp.broadcast_to(cw_ref[w:w + 1, :], (SUBLANES, D_CONV))

    def glu(u):
        a = u[:, :D_CONV].astype(F32)
        gate = u[:, D_CONV:].astype(F32)
        return a * jax.nn.sigmoid(gate)

    gh = jnp.where(step > 0, glu(uh_ref[0]), 0.0)
    g = jnp.concatenate([gh, glu(u_ref[0])], axis=0).astype(BF16)
    gext_ref[...] = jnp.dot(pext_ref[...], g, preferred_element_type=F32)
    first = CONV_HALO - (CONV_WIDTH - 1)
    rows = CONV_CHUNK

    def chunk(c, _):
        r0 = pl.multiple_of(c * rows, rows)
        acc = jnp.zeros((rows // SUBLANES, SUBLANES, D_CONV), F32)
        for w in range(CONV_WIDTH):
            start = pl.multiple_of(r0 + (first + w) * SUBLANES, SUBLANES)
            gw = gext_ref[pl.ds(start, rows), :].reshape(rows // SUBLANES, SUBLANES, D_CONV)
            acc = acc + gw * cwb_ref[w]
        yp_ref[pl.ds(r0, rows), :] = acc.reshape(rows, D_CONV)
        return 0

    lax.fori_loop(0, ts // rows, chunk, 0)
    y = yp_ref[...] + cb_ref[...]
    mu = jnp.mean(y, axis=-1, keepdims=True)
    d = y - mu
    var = jnp.mean(d * d, axis=-1, keepdims=True)
    yn = d * lax.rsqrt(var + EPS) * lg_ref[...] + lb_ref[...]
    act = (yn * jax.nn.sigmoid(yn)).astype(BF16)
    o_ref[0] = jnp.dot(pinv_ref[...], act, preferred_element_type=F32).astype(BF16)


def _conv_branch(u3, conv_w, conv_b, ln_g, ln_b):
    B, S, _ = u3.shape
    ts = TS_CONV
    halo_per_step = ts // CONV_HALO
    vec = lambda a: a.reshape(1, D_CONV)
    cw = jnp.concatenate([conv_w, jnp.zeros((CONV_HALO - CONV_WIDTH, D_CONV), F32)], axis=0)
    const = lambda rows: pl.BlockSpec((rows, D_CONV), lambda b, i: (0, 0))
    return pl.pallas_call(
        _conv_kernel,
        grid=(B, S // ts),
        in_specs=[pl.BlockSpec((1, ts, 2 * D_CONV), lambda b, i: (b, i, 0)),
                  pl.BlockSpec((1, CONV_HALO, 2 * D_CONV),
                               lambda b, i: (b, jnp.maximum(i * halo_per_step - 1, 0), 0)),
                  const(CONV_HALO), const(1), const(1), const(1)],
        out_specs=pl.BlockSpec((1, ts, D_CONV), lambda b, i: (b, i, 0)),
        out_shape=jax.ShapeDtypeStruct((B, S, D_CONV), BF16),
        scratch_shapes=[pltpu.VMEM((CONV_EXT_ROWS, ts + CONV_HALO), BF16),
                        pltpu.VMEM((ts, ts), BF16),
                        pltpu.VMEM((CONV_HALO, SUBLANES, D_CONV), F32),
                        pltpu.VMEM((CONV_EXT_ROWS, D_CONV), F32),
                        pltpu.VMEM((ts, D_CONV), F32)],
        compiler_params=_cparams(2),
        name="conv",
    )(u3, u3, cw, vec(conv_b), vec(ln_g), vec(ln_b))


def _memkv_kernel(m_ref, g_ref, w_ref, kt_ref, v_ref):
    mn = _rms(m_ref[0], g_ref[...]).astype(BF16)
    kv = jnp.dot(mn, w_ref[...], preferred_element_type=F32)
    kt_ref[0] = kv[:, :D_MODEL].T.astype(BF16)
    v_ref[0] = kv[:, D_MODEL:].astype(BF16)


def _memkv(mem, g, w):
    B, M, _ = mem.shape
    return pl.pallas_call(
        _memkv_kernel,
        grid=(B,),
        in_specs=[pl.BlockSpec((1, M, D_MODEL), lambda b: (b, 0, 0)),
                  pl.BlockSpec((1, D_MODEL), lambda b: (0, 0)),
                  pl.BlockSpec((D_MODEL, 2 * D_MODEL), lambda b: (0, 0))],
        out_specs=[pl.BlockSpec((1, D_MODEL, M), lambda b: (b, 0, 0)),
                   pl.BlockSpec((1, M, D_MODEL), lambda b: (b, 0, 0))],
        out_shape=[jax.ShapeDtypeStruct((B, D_MODEL, M), BF16),
                   jax.ShapeDtypeStruct((B, M, D_MODEL), BF16)],
        compiler_params=_cparams(1),
        name="memkv",
    )(mem, g.reshape(1, D_MODEL), w)


def _post_kernel(*refs):
    cnt_ref = refs[-1]

    @pl.when(jnp.logical_and(pl.program_id(0) == 0, pl.program_id(1) == 0))
    def _():
        cnt_ref[...] = jnp.zeros_like(cnt_ref)

    groups = [_post_rows(part, *refs) for part in range(POST_SPLIT)]
    live = True
    step = 0
    while live:
        live = False
        for part, group in enumerate(groups):
            if step >= part and next(group, None) is not None:
                live = True
        step += 1


def _post_rows(part, x_ref, sb_ref, cv_ref, wo_ref, gx_ref, wq_ref, kt_ref, vx_ref, wxo_ref, gf_ref,
               wr_ref, br_ref, x2_ref, hp_ref, ri_ref, cnt_ref):
    tm = TM_POST // POST_SPLIT
    rows = slice(part * tm, (part + 1) * tm)

    x1 = (x_ref[0, rows, :]
          + jnp.dot(sb_ref[0, rows, :], wo_ref[:D_SB, :], preferred_element_type=F32)
          + jnp.dot(cv_ref[0, rows, :], wo_ref[D_SB:, :], preferred_element_type=F32))
    yield 1

    hq = _rms(x1, gx_ref[...]).astype(BF16)
    yield 2
    q = (jnp.dot(hq, wq_ref[...], preferred_element_type=F32) * (XHEAD_DIM ** -0.5)).astype(BF16)
    yield 3
    outs = []
    for h in range(N_XHEADS):
        hs = slice(h * XHEAD_DIM, (h + 1) * XHEAD_DIM)
        s = jnp.dot(q[:, hs], kt_ref[0, hs, :], preferred_element_type=F32)
        s = s - jnp.max(s, axis=-1, keepdims=True)
        p = jnp.exp(s)
        p = p / jnp.sum(p, axis=-1, keepdims=True)
        outs.append(jnp.dot(p.astype(BF16), vx_ref[0, :, hs], preferred_element_type=F32))
    o = jnp.concatenate(outs, axis=1).astype(BF16)
    yield 4
    x2 = x1 + jnp.dot(o, wxo_ref[...], preferred_element_type=F32)
    x2_ref[0, rows, :] = x2
    yield 5

    hn = _rms(x2, gf_ref[...])
    _store_token_rows(hp_ref, part * tm, _pack_rows(hn))
    yield 6
    logits = jnp.dot(hn.astype(BF16), wr_ref[...], preferred_element_type=F32) + br_ref[...]
    yield 7
    lane = lax.broadcasted_iota(I32, (tm, LANES), 1)
    big = jnp.int32(LANES)
    ninf = jnp.float32(-jnp.inf)

    def first_argmax(vals):
        m = jnp.max(vals, axis=-1, keepdims=True)
        idx = jnp.min(jnp.where(vals == m, lane, big), axis=-1, keepdims=True)
        return m, idx

    is_group = lane < N_GROUPS
    gl = jnp.where(is_group, logits, ninf)
    gmax, gidx = first_argmax(gl)
    gsum = jnp.sum(jnp.where(is_group, jnp.exp(logits - gmax), 0.0), axis=-1, keepdims=True)
    g_w = 1.0 / gsum
    lo = RLANE_E + gidx * EXPERTS_PER_GROUP
    in_group = jnp.logical_and(lane >= lo, lane < lo + EXPERTS_PER_GROUP)
    el = jnp.where(in_group, logits, ninf)
    v1, i1 = first_argmax(el)
    el2 = jnp.where(lane == i1, ninf, el)
    v2, i2 = first_argmax(el2)
    t = jnp.exp(v2 - v1)
    w1 = g_w / (1.0 + t)
    w2 = g_w * t / (1.0 + t)
    yield 8

    hit1 = lane == i1
    hit2 = lane == i2
    oh = jnp.where(jnp.logical_or(hit1, hit2), 1.0, 0.0)
    rr = lax.broadcasted_iota(I32, (tm, tm), 0)
    cc = lax.broadcasted_iota(I32, (tm, tm), 1)
    ltri = jnp.where(cc < rr, 1.0, 0.0).astype(BF16)
    before = jnp.dot(ltri, oh.astype(BF16), preferred_element_type=F32) + cnt_ref[0:1, :]
    rank1 = jnp.sum(jnp.where(hit1, before, 0.0), axis=-1, keepdims=True)
    rank2 = jnp.sum(jnp.where(hit2, before, 0.0), axis=-1, keepdims=True)
    cnt_ref[...] = cnt_ref[...] + jnp.sum(oh, axis=0, keepdims=True)

    e1 = (i1 - RLANE_E).astype(F32)
    e2 = (i2 - RLANE_E).astype(F32)
    cols = (e1, e2, w1, w2, rank1, rank2)
    info = jnp.zeros((tm, LANES), F32)
    for c, val in enumerate(cols):
        info = jnp.where(lane == c, val, info)
    ri_ref[rows, :] = info


def _post(x, sb, cv, w_out, gx, w_xq, kt, vx, w_xo, gf, w_router, b_router):
    B, S, _ = x.shape
    T = B * S
    tm = TM_POST
    spt = S // tm
    tok = lambda width: pl.BlockSpec((1, tm, width), lambda b, i: (b, i, 0))
    flat = lambda width: pl.BlockSpec((tm, width), lambda b, i: (b * spt + i, 0))
    const = lambda r, c: pl.BlockSpec((r, c), lambda b, i: (0, 0))
    M = vx.shape[1]
    return pl.pallas_call(
        _post_kernel,
        grid=(B, spt),
        in_specs=[tok(D_MODEL), tok(D_SB), tok(D_CONV),
                  const(D_MODEL, D_MODEL), const(1, D_MODEL), const(D_MODEL, D_MODEL),
                  pl.BlockSpec((1, D_MODEL, M), lambda b, i: (b, 0, 0)),
                  pl.BlockSpec((1, M, D_MODEL), lambda b, i: (b, 0, 0)),
                  const(D_MODEL, D_MODEL), const(1, D_MODEL),
                  const(D_MODEL, LANES), const(1, LANES)],
        out_specs=[tok(D_MODEL), pl.BlockSpec((tm * ROW_CHUNKS, LANES), lambda b, i: (b * spt + i, 0)), flat(LANES),
                   pl.BlockSpec((SUBLANES, LANES), lambda b, i: (0, 0))],
        out_shape=[jax.ShapeDtypeStruct((B, S, D_MODEL), F32),
                   jax.ShapeDtypeStruct((T * ROW_CHUNKS, LANES), U32),
                   jax.ShapeDtypeStruct((T, LANES), F32),
                   jax.ShapeDtypeStruct((SUBLANES, LANES), F32)],
        compiler_params=_cparams(2),
        name="post",
    )(x, sb, cv, w_out, gx.reshape(1, D_MODEL), w_xq, kt, vx, w_xo, gf.reshape(1, D_MODEL),
      w_router, b_router)


def _token_rows(ref, token):
    return ref.at[pl.ds(pl.multiple_of(token * ROW_CHUNKS, ROW_CHUNKS), ROW_CHUNKS)]


def _row_copy_wait(src_rows, dst_rows, sem, n):
    pltpu.make_async_copy(src_rows.at[pl.ds(0, n * ROW_CHUNKS)], dst_rows.at[pl.ds(0, n * ROW_CHUNKS)], sem).wait()


def _dispatch_kernel(pend_ref, dest_ref, hp_ref, xd_ref, idx_ref, zero_ref, sem_idx, sem_rows, sem_zero):
    tm = TM_ROWS
    i = pl.program_id(0)
    n = pl.num_programs(0)
    slot = i % 2

    def idx_copy(step, s):
        return pltpu.make_async_copy(dest_ref.at[step], idx_ref.at[s], sem_idx.at[s])

    def zero_copy(e):
        start = pl.multiple_of((pend_ref[e] - TM_EXP) * ROW_CHUNKS, TM_EXP * ROW_CHUNKS)
        return pltpu.make_async_copy(zero_ref, xd_ref.at[pl.ds(start, TM_EXP * ROW_CHUNKS)], sem_zero)

    def nonempty(e):
        return pend_ref[e] > (pend_ref[e - 1] if e else 0)

    @pl.when(i == 0)
    def _():
        idx_copy(0, 0).start()
        zero_ref[...] = jnp.zeros_like(zero_ref)
        for e in range(N_EXPERTS):
            pl.when(nonempty(e))(lambda e=e: zero_copy(e).start())
        for e in range(N_EXPERTS):
            pl.when(nonempty(e))(lambda e=e: zero_copy(e).wait())

        def zero_unused_tile(j, _):
            start = pl.multiple_of(j * (TM_EXP * ROW_CHUNKS), TM_EXP * ROW_CHUNKS)
            cp = pltpu.make_async_copy(zero_ref, xd_ref.at[pl.ds(start, TM_EXP * ROW_CHUNKS)], sem_zero)
            cp.start()
            cp.wait()
            return 0

        n_tiles = xd_ref.shape[0] // (TM_EXP * ROW_CHUNKS)
        lax.fori_loop(pend_ref[N_EXPERTS - 1] // TM_EXP, n_tiles, zero_unused_tile, 0)

    idx_copy(i, slot).wait()

    @pl.when(i + 1 < n)
    def _():
        idx_copy(i + 1, 1 - slot).start()

    for r0 in range(0, tm, ISSUE_BATCH):
        batch = [(r, idx_ref[slot, k * tm + r]) for r in range(r0, r0 + ISSUE_BATCH) for k in range(2)]
        for r, d in batch:
            pltpu.make_async_copy(_token_rows(hp_ref, r), _token_rows(xd_ref, d), sem_rows).start()
    for k in range(2):
        _row_copy_wait(hp_ref, xd_ref, sem_rows, tm)


def _dispatch(pend, dest, hp, n_rows):
    T = hp.shape[0] // ROW_CHUNKS
    tm = TM_ROWS
    grid_spec = pltpu.PrefetchScalarGridSpec(
        num_scalar_prefetch=1,
        grid=(T // tm,),
        in_specs=[pl.BlockSpec(memory_space=pl.ANY),
                  pl.BlockSpec((tm * ROW_CHUNKS, LANES), lambda i, pend: (i, 0))],
        out_specs=pl.BlockSpec(memory_space=pl.ANY),
        scratch_shapes=[pltpu.SMEM((2, 2 * tm), I32), pltpu.VMEM((TM_EXP * ROW_CHUNKS, LANES), U32),
                        pltpu.SemaphoreType.DMA((2,)), pltpu.SemaphoreType.DMA, pltpu.SemaphoreType.DMA],
    )
    return pl.pallas_call(
        _dispatch_kernel,
        grid_spec=grid_spec,
        out_shape=jax.ShapeDtypeStruct((n_rows * ROW_CHUNKS, LANES), U32),
        compiler_params=_cparams(1),
        name="dispatch",
    )(pend, dest, hp)


def _experts_kernel(te_ref, nv_ref, xd_ref, wg_ref, wu_ref, wdn_ref, yd_ref, wgu_ref, wd_ref):
    i = pl.program_id(0)
    sub_rows = TM_EXP // EXP_SPLIT
    used = i < nv_ref[0]
    new_expert = jnp.logical_or(i == 0, te_ref[i] != te_ref[jnp.maximum(i - 1, 0)])

    @pl.when(jnp.logical_and(used, new_expert))
    def _():
        wgu_ref[:, :D_EXPERT] = wg_ref[0].astype(BF16)
        wgu_ref[:, D_EXPERT:] = wu_ref[0].astype(BF16)
        wd_ref[...] = wdn_ref[0].astype(BF16)

    def rows_pass(part):
        lo, hi = _unpack_rows(_load_token_rows(xd_ref, part * sub_rows, sub_rows))
        lo, hi = lo.astype(BF16), hi.astype(BF16)
        yield 1
        gu = (jnp.dot(lo, wgu_ref[:HALF, :], preferred_element_type=F32)
              + jnp.dot(hi, wgu_ref[HALF:, :], preferred_element_type=F32))
        yield 2
        gate = gu[:, :D_EXPERT]
        up = gu[:, D_EXPERT:]
        hmid = (gate * jax.nn.sigmoid(gate) * up).astype(BF16)
        yield 3
        y = jnp.dot(hmid, wd_ref[...], preferred_element_type=F32)
        yield 4
        _store_token_rows(yd_ref, part * sub_rows, _pack_rows(y))

    @pl.when(jnp.logical_not(used))
    def _():
        yd_ref[...] = jnp.zeros_like(yd_ref)

    @pl.when(used)
    def _():
        groups = [rows_pass(part) for part in range(EXP_SPLIT)]
        live = True
        step = 0
        while live:
            live = False
            for part, group in enumerate(groups):
                if step >= part and next(group, None) is not None:
                    live = True
            step += 1


def _experts(tile_expert, n_valid, xd, w_gate, w_up, w_down):
    n_tiles = tile_expert.shape[0]
    tm = TM_EXP
    tile = lambda i, te, nv: (jnp.minimum(i, nv[0] - 1), 0)
    weight = lambda i, te, nv: (te[jnp.minimum(i, nv[0] - 1)], 0, 0)
    grid_spec = pltpu.PrefetchScalarGridSpec(
        num_scalar_prefetch=2,
        grid=(n_tiles,),
        in_specs=[pl.BlockSpec((tm * ROW_CHUNKS, LANES), tile),
                  pl.BlockSpec((1, D_MODEL, D_EXPERT), weight),
                  pl.BlockSpec((1, D_MODEL, D_EXPERT), weight),
                  pl.BlockSpec((1, D_EXPERT, D_MODEL), weight)],
        out_specs=pl.BlockSpec((tm * ROW_CHUNKS, LANES), lambda i, te, nv: (i, 0)),
        scratch_shapes=[pltpu.VMEM((D_MODEL, 2 * D_EXPERT), BF16), pltpu.VMEM((D_EXPERT, D_MODEL), BF16)],
    )
    return pl.pallas_call(
        _experts_kernel,
        grid_spec=grid_spec,
        out_shape=jax.ShapeDtypeStruct((n_tiles * tm * ROW_CHUNKS, LANES), U32),
        compiler_params=_cparams(1),
        name="experts",
    )(tile_expert, n_valid, xd, w_gate, w_up, w_down)


def _combine_kernel(dest_ref, yd_ref, x2_ref, ri_ref, g_ref, o_ref, idx_ref, rows_ref, sem_idx, sem_rows):
    tm = TM_ROWS
    i = pl.program_id(0)
    n = pl.num_programs(0)
    slot = i % 2

    def fetch(step, s):
        cp = pltpu.make_async_copy(dest_ref.at[step], idx_ref.at[s], sem_idx)
        cp.start()
        cp.wait()
        for r0 in range(0, tm, ISSUE_BATCH):
            batch = [(r, k, idx_ref[s, k * tm + r]) for r in range(r0, r0 + ISSUE_BATCH) for k in range(2)]
            for r, k, d in batch:
                pltpu.make_async_copy(_token_rows(yd_ref, d), _token_rows(rows_ref.at[s, k], r),
                                      sem_rows.at[s]).start()

    @pl.when(i == 0)
    def _():
        fetch(0, 0)

    @pl.when(i + 1 < n)
    def _():
        fetch(i + 1, 1 - slot)

    for k in range(2):
        _row_copy_wait(yd_ref, rows_ref.at[slot, k], sem_rows.at[slot], tm)

    ri = ri_ref[...]
    x2 = x2_ref[...]
    y_lo = jnp.zeros((tm, HALF), F32)
    y_hi = jnp.zeros((tm, HALF), F32)
    for k in range(2):
        lo, hi = _unpack_rows(_load_token_rows(rows_ref.at[slot, k], 0, tm))
        wk = ri[:, 2 + k:3 + k]
        y_lo = y_lo + wk * lo
        y_hi = y_hi + wk * hi
    x3 = x2 + jnp.concatenate([y_lo, y_hi], axis=1)
    o_ref[...] = _rms(x3, g_ref[...])


def _combine(dest, yd, x2, rinfo, g):
    T = x2.shape[0]
    tm = TM_ROWS
    return pl.pallas_call(
        _combine_kernel,
        grid=(T // tm,),
        in_specs=[pl.BlockSpec(memory_space=pl.ANY),
                  pl.BlockSpec(memory_space=pl.ANY),
                  pl.BlockSpec((tm, D_MODEL), lambda i: (i, 0)),
                  pl.BlockSpec((tm, LANES), lambda i: (i, 0)),
                  pl.BlockSpec((1, D_MODEL), lambda i: (0, 0))],
        out_specs=pl.BlockSpec((tm, D_MODEL), lambda i: (i, 0)),
        out_shape=jax.ShapeDtypeStruct((T, D_MODEL), F32),
        scratch_shapes=[pltpu.SMEM((2, 2 * tm), I32), pltpu.VMEM((2, 2, tm * ROW_CHUNKS, LANES), U32),
                        pltpu.SemaphoreType.DMA, pltpu.SemaphoreType.DMA((2,))],
        compiler_params=_cparams(1),
        name="combine",
    )(dest, yd, x2, rinfo, g.reshape(1, D_MODEL))


def _routing_tables(rinfo, counts_row, T):
    tm = TM_EXP
    counts = counts_row[RLANE_E:RLANE_E + N_EXPERTS].astype(I32)
    padded = (counts + tm - 1) // tm * tm
    pend = jnp.cumsum(padded)
    pstart = pend - padded
    e = rinfo[:, 0:2].astype(I32)
    rank = rinfo[:, 4:6].astype(I32)
    ids = jnp.arange(N_EXPERTS, dtype=I32)
    dest = rank + jnp.sum(jnp.where(e[..., None] == ids, pstart, 0), axis=-1)
    n_tiles = (2 * T) // tm + N_EXPERTS
    tile_start = jnp.arange(n_tiles, dtype=I32) * tm
    tile_expert = jnp.minimum(jnp.sum((tile_start[:, None] >= pend[None, :]).astype(I32), axis=1),
                              N_EXPERTS - 1)
    n_valid = (pend[-1:] // tm).astype(I32)
    ntt = T // TM_ROWS
    dest_tiles = dest.reshape(ntt, TM_ROWS, 2).transpose(0, 2, 1).reshape(ntt, 2 * TM_ROWS)
    return dest_tiles, tile_expert, n_valid, pend.astype(I32), n_tiles


def kernel(x, mem, ln_mix_g, w_in, sb_out_g, conv_w, conv_b, conv_ln_g, conv_ln_b, w_out, ln_mem_x_g,
           ln_mem_g, w_xq, w_xkv, w_xo, ln_ffn_g, w_group, b_group, w_er, b_er, w_gate, w_up, w_down,
           ln_final_g):
    B, S, D = x.shape
    T = B * S
    assert ln_mix_g.shape[0] == 1, "single-layer pipeline: the output norm is fused into the combine kernel"
    assert D == D_MODEL and S % TS_CONV == 0 and S % TM_PROJ == 0 and T % TM_ROWS == 0
    l = 0
    qt4, k5, vt5, u = _inproj(x, ln_mix_g[l], w_in[l].astype(BF16))
    sb = _sb_attention(qt4, k5, vt5, sb_out_g[l])
    cv = _conv_branch(u, conv_w[l], conv_b[l], conv_ln_g[l], conv_ln_b[l])
    kt, vx = _memkv(mem, ln_mem_g[l], w_xkv[l].astype(BF16))

    pad = LANES - RLANE_E - N_EXPERTS
    w_router = jnp.concatenate([w_group[l], w_er[l].reshape(D, N_EXPERTS), jnp.zeros((D, pad), F32)], axis=1)
    b_router = jnp.concatenate([b_group[l], b_er[l].reshape(N_EXPERTS), jnp.zeros((pad,), F32)])
    x2, hp, rinfo, cnt = _post(x, sb, cv, w_out[l].astype(BF16), ln_mem_x_g[l], w_xq[l].astype(BF16),
                               kt, vx, w_xo[l].astype(BF16), ln_ffn_g[l],
                               w_router.astype(BF16), b_router.reshape(1, LANES))

    dest_tiles, tile_expert, n_valid, pend, n_tiles = _routing_tables(rinfo, cnt[0], T)
    xd = _dispatch(pend, dest_tiles, hp, n_tiles * TM_EXP)
    yd = _experts(tile_expert, n_valid, xd, w_gate[l], w_up[l], w_down[l])
    return _combine(dest_tiles, yd, x2.reshape(T, D), rinfo, ln_final_g).reshape(B, S, D)
```

```python
import functools

import jax
import jax.numpy as jnp
from jax import lax
from jax.experimental import pallas as pl
from jax.experimental.pallas import tpu as pltpu

F32 = jnp.float32
BF16 = jnp.bfloat16
I32 = jnp.int32
U32 = jnp.uint32

EPS = 1e-6

D_MODEL = 1024
N_SB_HEADS = 8
SB_HEAD_DIM = 64
D_SB = N_SB_HEADS * SB_HEAD_DIM
D_CONV = D_MODEL - D_SB
CONV_WIDTH = 31
N_XHEADS = 4
XHEAD_DIM = D_MODEL // N_XHEADS
N_GROUPS = 4
EXPERTS_PER_GROUP = 8
N_EXPERTS = N_GROUPS * EXPERTS_PER_GROUP
D_EXPERT = 512

LANES = 128
SUBLANES = 8
VMEM_LIMIT = 56 * 1024 * 1024

TM_PROJ = 512
SB_BLK = 256
SB_GROUP = SB_BLK // SUBLANES
TS_CONV = 512
CONV_HALO = 32
CONV_CHUNK = 32
CONV_L = TS_CONV // SUBLANES
CONV_EXT_ROWS = (CONV_L + CONV_HALO) * SUBLANES
CONV_PITCH_IN = 100
CONV_PITCH_OUT = 68
TM_POST = 512
POST_SPLIT = 2
TM_ROWS = 512
TM_EXP = 512
EXP_SPLIT = 2
ISSUE_BATCH = 8
HALF = D_MODEL // 2
ROW_CHUNKS = HALF // LANES
RLANE_E = 4
NEG_BIG = -1e30
LOG2E = 1.4426950408889634
LOGIT_CAP = 64.0


def _cparams(n_axes):
    return pltpu.CompilerParams(dimension_semantics=("arbitrary",) * n_axes,
                                vmem_limit_bytes=VMEM_LIMIT)


def _rms(x, g):
    ms = jnp.mean(x * x, axis=-1, keepdims=True)
    return x * lax.rsqrt(ms + EPS) * g


def _pack_rows(y):
    lo = pltpu.bitcast(y[:, :HALF].astype(BF16).astype(F32), U32)
    hi = pltpu.bitcast(y[:, HALF:].astype(BF16).astype(F32), U32)
    return (lo >> 16) | (hi & jnp.uint32(0xFFFF0000))


def _unpack_rows(w):
    lo = pltpu.bitcast(w << 16, F32)
    hi = pltpu.bitcast(w & jnp.uint32(0xFFFF0000), F32)
    return lo, hi


def _store_token_rows(ref, first, words):
    rows = words.shape[0]
    for c in range(ROW_CHUNKS):
        ref[pl.ds(first * ROW_CHUNKS + c, rows, stride=ROW_CHUNKS), :] = words[:, c * LANES:(c + 1) * LANES]


def _load_token_rows(ref, first, rows):
    return jnp.concatenate([ref[pl.ds(first * ROW_CHUNKS + c, rows, stride=ROW_CHUNKS), :]
                            for c in range(ROW_CHUNKS)], axis=1)


def _inproj_kernel(x_ref, g_ref, w_ref, qt_ref, k_ref, vt_ref, u_ref, perm_ref):
    blk = SB_BLK
    npair = N_SB_HEADS // 2

    @pl.when(jnp.logical_and(pl.program_id(0) == 0, pl.program_id(1) == 0))
    def _():
        p = lax.broadcasted_iota(I32, (blk, blk), 0)
        c = lax.broadcasted_iota(I32, (blk, blk), 1)
        perm_ref[...] = jnp.where(c == (p & (SUBLANES - 1)) * SB_GROUP + (p >> 3), 1.0, 0.0).astype(BF16)

    h = _rms(x_ref[0], g_ref[...]).astype(BF16)
    p = jnp.dot(h, w_ref[...], preferred_element_type=F32)
    u_ref[0] = p[:, 3 * D_SB:].astype(BF16)
    q = p[:, :D_SB] * (SB_HEAD_DIM ** -0.5 * LOG2E)
    for j in range(npair):
        qt_ref[0, j] = q[:, j * LANES:(j + 1) * LANES].T.astype(BF16)
    for nb in range(TM_PROJ // blk):
        rows = slice(nb * blk, (nb + 1) * blk)
        kv = p[rows, D_SB:3 * D_SB].astype(BF16)
        kvp = jnp.dot(perm_ref[...], kv, preferred_element_type=F32)
        for j in range(npair):
            k_ref[0, j, nb] = kvp[:, j * LANES:(j + 1) * LANES].astype(BF16)
            vt = kvp[:, D_SB + j * LANES:D_SB + (j + 1) * LANES].T.astype(BF16)
            vt_ref[0, 2 * j, nb] = vt[:SB_HEAD_DIM]
            vt_ref[0, 2 * j + 1, nb] = vt[SB_HEAD_DIM:]


def _inproj(x, g, w):
    B, S, _ = x.shape
    n_out = w.shape[1]
    tm = TM_PROJ
    blk = SB_BLK
    nb = tm // blk
    npair = N_SB_HEADS // 2
    return pl.pallas_call(
        _inproj_kernel,
        grid=(B, S // tm),
        in_specs=[pl.BlockSpec((1, tm, D_MODEL), lambda b, i: (b, i, 0)),
                  pl.BlockSpec((1, D_MODEL), lambda b, i: (0, 0)),
                  pl.BlockSpec((D_MODEL, n_out), lambda b, i: (0, 0))],
        out_specs=[pl.BlockSpec((1, npair, LANES, tm), lambda b, i: (b, 0, 0, i)),
                   pl.BlockSpec((1, npair, nb, blk, LANES), lambda b, i: (b, 0, i, 0, 0)),
                   pl.BlockSpec((1, N_SB_HEADS, nb, SB_HEAD_DIM, blk), lambda b, i: (b, 0, i, 0, 0)),
                   pl.BlockSpec((1, tm, 2 * D_CONV), lambda b, i: (b, i, 0))],
        out_shape=[jax.ShapeDtypeStruct((B, npair, LANES, S), BF16),
                   jax.ShapeDtypeStruct((B, npair, S // blk, blk, LANES), BF16),
                   jax.ShapeDtypeStruct((B, N_SB_HEADS, S // blk, SB_HEAD_DIM, blk), BF16),
                   jax.ShapeDtypeStruct((B, S, 2 * D_CONV), BF16)],
        scratch_shapes=[pltpu.VMEM((blk, blk), BF16)],
        compiler_params=_cparams(2),
        name="inproj",
    )(x, g.reshape(1, D_MODEL), w)


def _sb_kernel(qt_ref, k_ref, vt_ref, g_ref, o_ref, ot_ref, carry_ref, qmt_ref, bias_ref,
               z0, z1, z2, e0, e1, e2, w0, w1, w2, at0, at1, at2):
    qi = pl.program_id(1)
    blk = SB_BLK
    n_items = N_SB_HEADS * (qi + 1)

    @pl.when(jnp.logical_and(pl.program_id(0) == 0, qi == 0))
    def _():
        rowp = lax.broadcasted_iota(I32, (blk, blk), 0)
        colc = lax.broadcasted_iota(I32, (blk, blk), 1)
        key_local = (rowp & (SUBLANES - 1)) * SB_GROUP + (rowp >> 3)
        bias_ref[0] = jnp.zeros((blk, blk), F32)
        bias_ref[1] = jnp.where(key_local < colc, 0.0, NEG_BIG)
        bias_ref[2] = jnp.full((blk, blk), NEG_BIG, F32)

    ot_ref[...] = jnp.zeros_like(ot_ref)
    carry_ref[...] = jnp.ones_like(carry_ref)
    at1[...] = jnp.zeros_like(at1)
    at2[...] = jnp.zeros_like(at2)
    e2[...] = jnp.zeros_like(e2)
    w2[...] = jnp.zeros_like(w2)
    rowi = lax.broadcasted_iota(I32, (LANES, blk), 0)
    for j in range(N_SB_HEADS // 2):
        qs = qt_ref[0, j]
        zero = jnp.zeros_like(qs)
        qmt_ref[2 * j] = jnp.where(rowi < SB_HEAD_DIM, qs, zero)
        qmt_ref[2 * j + 1] = jnp.where(rowi >= SB_HEAD_DIM, qs, zero)
    sub = lax.broadcasted_iota(I32, (SUBLANES, blk), 0)

    def item(m):
        mc = jnp.minimum(m, n_items - 1)
        return qi - (mc >> 3), mc & (N_SB_HEADS - 1)

    def z_phase(m, zw):
        kb, head = item(m)
        zw[...] = jnp.dot(k_ref[0, head >> 1, kb], qmt_ref[head], preferred_element_type=F32)

    def e1_phase(m, zr, er, wr):
        kb, head = item(m)
        bidx = jnp.where(m >= n_items, 2, jnp.where(kb == qi, 1, 0))
        pex = jnp.ones((SUBLANES, blk), F32)
        for i in reversed(range(SB_GROUP)):
            rows = slice(SUBLANES * i, SUBLANES * (i + 1))
            z = zr[rows, :] + bias_ref[bidx, rows, :]
            t = jnp.exp2(jnp.minimum(z, LOGIT_CAP))
            keep = 1.0 / (1.0 + t)
            er[rows, :] = (t * keep) * pex
            pex = pex * keep
        incl = pex
        for s in (1, 2, 4):
            shifted = pltpu.roll(incl, SUBLANES - s, 0)
            incl = incl * jnp.where(sub + s < SUBLANES, shifted, 1.0)
        later = jnp.where(sub + 1 < SUBLANES, pltpu.roll(incl, SUBLANES - 1, 0), 1.0)
        carry = carry_ref[head]
        w = carry * later
        carry_ref[head] = carry * jnp.broadcast_to(incl[0:1], (SUBLANES, blk))
        wr[...] = jnp.concatenate([w, w], axis=0)

    def e2_phase(er, wr, atw):
        scale = wr[...]
        for i in range(0, SB_GROUP, 2):
            rows = slice(SUBLANES * i, SUBLANES * (i + 2))
            atw[rows, :] = (er[rows, :] * scale).astype(BF16)

    def av_phase(m, atr):
        kb, head = item(jnp.maximum(m - 2, 0))
        ot_ref[head] = ot_ref[head] + jnp.dot(vt_ref[0, head, kb], atr[...], preferred_element_type=F32)

    zs, es, ws, ats = (z0, z1, z2), (e0, e1, e2), (w0, w1, w2), (at0, at1, at2)
    z_phase(0, z0)
    z_phase(1, z1)

    def body(t, _):
        for u in range(3):
            m = 3 * t + u
            z_phase(m + 2, zs[(u + 2) % 3])
            e1_phase(m, zs[u], es[u], ws[u])
            e2_phase(es[(u + 2) % 3], ws[(u + 2) % 3], ats[(u + 2) % 3])
            av_phase(m, ats[(u + 1) % 3])
        return 0

    lax.fori_loop(0, (n_items + 4) // 3, body, 0)
    parts = [ot_ref[2 * j:2 * j + 2].reshape(LANES, blk).T for j in range(N_SB_HEADS // 2)]
    o_ref[0] = _rms(jnp.concatenate(parts, axis=1), g_ref[...]).astype(BF16)


def _sb_attention(qt4, k5, vt5, g):
    B, npair, _, S = qt4.shape
    nkb = S // SB_BLK
    blk = SB_BLK
    return pl.pallas_call(
        _sb_kernel,
        grid=(B, nkb),
        in_specs=[pl.BlockSpec((1, npair, LANES, blk), lambda b, i: (b, 0, 0, i)),
                  pl.BlockSpec((1, npair, nkb, blk, LANES), lambda b, i: (b, 0, 0, 0, 0)),
                  pl.BlockSpec((1, N_SB_HEADS, nkb, SB_HEAD_DIM, blk), lambda b, i: (b, 0, 0, 0, 0)),
                  pl.BlockSpec((1, D_SB), lambda b, i: (0, 0))],
        out_specs=pl.BlockSpec((1, blk, D_SB), lambda b, i: (b, i, 0)),
        out_shape=jax.ShapeDtypeStruct((B, S, D_SB), BF16),
        scratch_shapes=[pltpu.VMEM((N_SB_HEADS, SB_HEAD_DIM, blk), F32),
                        pltpu.VMEM((N_SB_HEADS, SUBLANES, blk), F32),
                        pltpu.VMEM((N_SB_HEADS, LANES, blk), BF16),
                        pltpu.VMEM((3, blk, blk), F32),
                        ] + [pltpu.VMEM((blk, blk), F32)] * 6
                        + [pltpu.VMEM((2 * SUBLANES, blk), F32)] * 3
                        + [pltpu.VMEM((blk, blk), BF16)] * 3,
        compiler_params=_cparams(2),
        name="sb_attn",
    )(qt4, k5, vt5, g.reshape(1, D_SB))


def _conv_kernel(u_ref, uh_ref, cw_ref, cb_ref, lg_ref, lb_ref, o_ref, cwb_ref, gs_ref, gext_ref, yp_ref, os_ref):
    step = pl.program_id(1)
    ts = TS_CONV
    nslab = D_CONV // LANES
    seg = CONV_L + CONV_HALO

    @pl.when(jnp.logical_and(pl.program_id(0) == 0, step == 0))
    def _():
        for w in range(CONV_WIDTH):
            cwb_ref[w] = jnp.broadcast_to(cw_ref[w:w + 1, :], (SUBLANES, D_CONV))

    def glu(u):
        a = u[:, :D_CONV].astype(F32)
        gate = u[:, D_CONV:].astype(F32)
        return a * jax.nn.sigmoid(gate)

    gh = jnp.where(step > 0, glu(uh_ref[0]), 0.0)
    g = jnp.concatenate([gh, glu(u_ref[0])], axis=0)
    for r in range(SUBLANES):
        for l in range(nslab):
            gs_ref[l, r * CONV_PITCH_IN:r * CONV_PITCH_IN + seg, :] = (
                g[r * CONV_L:r * CONV_L + seg, l * LANES:(l + 1) * LANES])
    for e in range(seg):
        for l in range(nslab):
            gext_ref[e * SUBLANES:(e + 1) * SUBLANES, l * LANES:(l + 1) * LANES] = (
                gs_ref[l, pl.ds(e, SUBLANES, stride=CONV_PITCH_IN), :])
    first = CONV_HALO - (CONV_WIDTH - 1)
    rows = CONV_CHUNK

    def chunk(c, _):
        r0 = pl.multiple_of(c * rows, rows)
        acc = jnp.zeros((rows // SUBLANES, SUBLANES, D_CONV), F32)
        for w in range(CONV_WIDTH):
            start = pl.multiple_of(r0 + (first + w) * SUBLANES, SUBLANES)
            gw = gext_ref[pl.ds(start, rows), :].reshape(rows // SUBLANES, SUBLANES, D_CONV)
            acc = acc + gw * cwb_ref[w]
        yp_ref[pl.ds(r0, rows), :] = acc.reshape(rows, D_CONV)
        return 0

    lax.fori_loop(0, ts // rows, chunk, 0)
    y = yp_ref[...] + cb_ref[...]
    mu = jnp.mean(y, axis=-1, keepdims=True)
    d = y - mu
    var = jnp.mean(d * d, axis=-1, keepdims=True)
    yn = d * lax.rsqrt(var + EPS) * lg_ref[...] + lb_ref[...]
    act = yn * jax.nn.sigmoid(yn)
    for i in range(CONV_L):
        for l in range(nslab):
            os_ref[l, pl.ds(i, SUBLANES, stride=CONV_PITCH_OUT), :] = (
                act[i * SUBLANES:(i + 1) * SUBLANES, l * LANES:(l + 1) * LANES])
    for r in range(SUBLANES):
        for l in range(nslab):
            o_ref[0, r * CONV_L:(r + 1) * CONV_L, l * LANES:(l + 1) * LANES] = (
                os_ref[l, r * CONV_PITCH_OUT:r * CONV_PITCH_OUT + CONV_L, :].astype(BF16))


def _conv_branch(u3, conv_w, conv_b, ln_g, ln_b):
    B, S, _ = u3.shape
    ts = TS_CONV
    halo_per_step = ts // CONV_HALO
    vec = lambda a: a.reshape(1, D_CONV)
    cw = jnp.concatenate([conv_w, jnp.zeros((CONV_HALO - CONV_WIDTH, D_CONV), F32)], axis=0)
    const = lambda rows: pl.BlockSpec((rows, D_CONV), lambda b, i: (0, 0))
    return pl.pallas_call(
        _conv_kernel,
        grid=(B, S // ts),
        in_specs=[pl.BlockSpec((1, ts, 2 * D_CONV), lambda b, i: (b, i, 0)),
                  pl.BlockSpec((1, CONV_HALO, 2 * D_CONV),
                               lambda b, i: (b, jnp.maximum(i * halo_per_step - 1, 0), 0)),
                  const(CONV_HALO), const(1), const(1), const(1)],
        out_specs=pl.BlockSpec((1, ts, D_CONV), lambda b, i: (b, i, 0)),
        out_shape=jax.ShapeDtypeStruct((B, S, D_CONV), BF16),
        scratch_shapes=[pltpu.VMEM((CONV_HALO, SUBLANES, D_CONV), F32),
                        pltpu.VMEM((D_CONV // LANES, SUBLANES * CONV_PITCH_IN, LANES), F32),
                        pltpu.VMEM((CONV_EXT_ROWS, D_CONV), F32),
                        pltpu.VMEM((ts, D_CONV), F32),
                        pltpu.VMEM((D_CONV // LANES, SUBLANES * CONV_PITCH_OUT, LANES), F32)],
        compiler_params=_cparams(2),
        name="conv",
    )(u3, u3, cw, vec(conv_b), vec(ln_g), vec(ln_b))


def _memkv_kernel(m_ref, g_ref, w_ref, kt_ref, v_ref):
    mn = _rms(m_ref[0], g_ref[...]).astype(BF16)
    kv = jnp.dot(mn, w_ref[...], preferred_element_type=F32)
    kt_ref[0] = kv[:, :D_MODEL].T.astype(BF16)
    v_ref[0] = kv[:, D_MODEL:].astype(BF16)


def _memkv(mem, g, w):
    B, M, _ = mem.shape
    return pl.pallas_call(
        _memkv_kernel,
        grid=(B,),
        in_specs=[pl.BlockSpec((1, M, D_MODEL), lambda b: (b, 0, 0)),
                  pl.BlockSpec((1, D_MODEL), lambda b: (0, 0)),
                  pl.BlockSpec((D_MODEL, 2 * D_MODEL), lambda b: (0, 0))],
        out_specs=[pl.BlockSpec((1, D_MODEL, M), lambda b: (b, 0, 0)),
                   pl.BlockSpec((1, M, D_MODEL), lambda b: (b, 0, 0))],
        out_shape=[jax.ShapeDtypeStruct((B, D_MODEL, M), BF16),
                   jax.ShapeDtypeStruct((B, M, D_MODEL), BF16)],
        compiler_params=_cparams(1),
        name="memkv",
    )(mem, g.reshape(1, D_MODEL), w)


def _post_kernel(*refs):
    cnt_ref = refs[-1]

    @pl.when(jnp.logical_and(pl.program_id(0) == 0, pl.program_id(1) == 0))
    def _():
        cnt_ref[...] = jnp.zeros_like(cnt_ref)

    groups = [_post_rows(part, *refs) for part in range(POST_SPLIT)]
    live = True
    step = 0
    while live:
        live = False
        for part, group in enumerate(groups):
            if step >= part and next(group, None) is not None:
                live = True
        step += 1


def _post_rows(part, x_ref, sb_ref, cv_ref, wo_ref, gx_ref, wq_ref, kt_ref, vx_ref, wxo_ref, gf_ref,
               wr_ref, br_ref, x2_ref, hp_ref, ri_ref, cnt_ref):
    tm = TM_POST // POST_SPLIT
    rows = slice(part * tm, (part + 1) * tm)

    x1 = (x_ref[0, rows, :]
          + jnp.dot(sb_ref[0, rows, :], wo_ref[:D_SB, :], preferred_element_type=F32)
          + jnp.dot(cv_ref[0, rows, :], wo_ref[D_SB:, :], preferred_element_type=F32))
    yield 1

    hq = _rms(x1, gx_ref[...]).astype(BF16)
    yield 2
    q = (jnp.dot(hq, wq_ref[...], preferred_element_type=F32) * (XHEAD_DIM ** -0.5)).astype(BF16)
    yield 3
    outs = []
    for h in range(N_XHEADS):
        hs = slice(h * XHEAD_DIM, (h + 1) * XHEAD_DIM)
        s = jnp.dot(q[:, hs], kt_ref[0, hs, :], preferred_element_type=F32)
        s = s - jnp.max(s, axis=-1, keepdims=True)
        p = jnp.exp(s)
        p = p / jnp.sum(p, axis=-1, keepdims=True)
        outs.append(jnp.dot(p.astype(BF16), vx_ref[0, :, hs], preferred_element_type=F32))
    o = jnp.concatenate(outs, axis=1).astype(BF16)
    yield 4
    x2 = x1 + jnp.dot(o, wxo_ref[...], preferred_element_type=F32)
    x2_ref[0, rows, :] = x2
    yield 5

    hn = _rms(x2, gf_ref[...])
    _store_token_rows(hp_ref, part * tm, _pack_rows(hn))
    yield 6
    logits = jnp.dot(hn.astype(BF16), wr_ref[...], preferred_element_type=F32) + br_ref[...]
    yield 7
    lane = lax.broadcasted_iota(I32, (tm, LANES), 1)
    big = jnp.int32(LANES)
    ninf = jnp.float32(-jnp.inf)

    def first_argmax(vals):
        m = jnp.max(vals, axis=-1, keepdims=True)
        idx = jnp.min(jnp.where(vals == m, lane, big), axis=-1, keepdims=True)
        return m, idx

    is_group = lane < N_GROUPS
    gl = jnp.where(is_group, logits, ninf)
    gmax, gidx = first_argmax(gl)
    gsum = jnp.sum(jnp.where(is_group, jnp.exp(logits - gmax), 0.0), axis=-1, keepdims=True)
    g_w = 1.0 / gsum
    lo = RLANE_E + gidx * EXPERTS_PER_GROUP
    in_group = jnp.logical_and(lane >= lo, lane < lo + EXPERTS_PER_GROUP)
    el = jnp.where(in_group, logits, ninf)
    v1, i1 = first_argmax(el)
    el2 = jnp.where(lane == i1, ninf, el)
    v2, i2 = first_argmax(el2)
    t = jnp.exp(v2 - v1)
    w1 = g_w / (1.0 + t)
    w2 = g_w * t / (1.0 + t)
    yield 8

    hit1 = lane == i1
    hit2 = lane == i2
    oh = jnp.where(jnp.logical_or(hit1, hit2), 1.0, 0.0)
    rr = lax.broadcasted_iota(I32, (tm, tm), 0)
    cc = lax.broadcasted_iota(I32, (tm, tm), 1)
    ltri = jnp.where(cc < rr, 1.0, 0.0).astype(BF16)
    before = jnp.dot(ltri, oh.astype(BF16), preferred_element_type=F32) + cnt_ref[0:1, :]
    rank1 = jnp.sum(jnp.where(hit1, before, 0.0), axis=-1, keepdims=True)
    rank2 = jnp.sum(jnp.where(hit2, before, 0.0), axis=-1, keepdims=True)
    cnt_ref[...] = cnt_ref[...] + jnp.sum(oh, axis=0, keepdims=True)

    e1 = (i1 - RLANE_E).astype(F32)
    e2 = (i2 - RLANE_E).astype(F32)
    cols = (e1, e2, w1, w2, rank1, rank2)
    info = jnp.zeros((tm, LANES), F32)
    for c, val in enumerate(cols):
        info = jnp.where(lane == c, val, info)
    ri_ref[rows, :] = info


def _post(x, sb, cv, w_out, gx, w_xq, kt, vx, w_xo, gf, w_router, b_router):
    B, S, _ = x.shape
    T = B * S
    tm = TM_POST
    spt = S // tm
    tok = lambda width: pl.BlockSpec((1, tm, width), lambda b, i: (b, i, 0))
    flat = lambda width: pl.BlockSpec((tm, width), lambda b, i: (b * spt + i, 0))
    const = lambda r, c: pl.BlockSpec((r, c), lambda b, i: (0, 0))
    M = vx.shape[1]
    return pl.pallas_call(
        _post_kernel,
        grid=(B, spt),
        in_specs=[tok(D_MODEL), tok(D_SB), tok(D_CONV),
                  const(D_MODEL, D_MODEL), const(1, D_MODEL), const(D_MODEL, D_MODEL),
                  pl.BlockSpec((1, D_MODEL, M), lambda b, i: (b, 0, 0)),
                  pl.BlockSpec((1, M, D_MODEL), lambda b, i: (b, 0, 0)),
                  const(D_MODEL, D_MODEL), const(1, D_MODEL),
                  const(D_MODEL, LANES), const(1, LANES)],
        out_specs=[tok(D_MODEL), pl.BlockSpec((tm * ROW_CHUNKS, LANES), lambda b, i: (b * spt + i, 0)), flat(LANES),
                   pl.BlockSpec((SUBLANES, LANES), lambda b, i: (0, 0))],
        out_shape=[jax.ShapeDtypeStruct((B, S, D_MODEL), F32),
                   jax.ShapeDtypeStruct((T * ROW_CHUNKS, LANES), U32),
                   jax.ShapeDtypeStruct((T, LANES), F32),
                   jax.ShapeDtypeStruct((SUBLANES, LANES), F32)],
        compiler_params=_cparams(2),
        name="post",
    )(x, sb, cv, w_out, gx.reshape(1, D_MODEL), w_xq, kt, vx, w_xo, gf.reshape(1, D_MODEL),
      w_router, b_router)


def _token_rows(ref, token):
    return ref.at[pl.ds(pl.multiple_of(token * ROW_CHUNKS, ROW_CHUNKS), ROW_CHUNKS)]


def _row_copy_wait(src_rows, dst_rows, sem, n):
    pltpu.make_async_copy(src_rows.at[pl.ds(0, n * ROW_CHUNKS)], dst_rows.at[pl.ds(0, n * ROW_CHUNKS)], sem).wait()


def _dispatch_kernel(pend_ref, dest_ref, hp_ref, xd_ref, idx_ref, zero_ref, sem_idx, sem_rows, sem_zero):
    tm = TM_ROWS
    i = pl.program_id(0)
    n = pl.num_programs(0)
    slot = i % 2

    def idx_copy(step, s):
        return pltpu.make_async_copy(dest_ref.at[step], idx_ref.at[s], sem_idx.at[s])

    def zero_copy(e):
        start = pl.multiple_of((pend_ref[e] - TM_EXP) * ROW_CHUNKS, TM_EXP * ROW_CHUNKS)
        return pltpu.make_async_copy(zero_ref, xd_ref.at[pl.ds(start, TM_EXP * ROW_CHUNKS)], sem_zero)

    def nonempty(e):
        return pend_ref[e] > (pend_ref[e - 1] if e else 0)

    @pl.when(i == 0)
    def _():
        idx_copy(0, 0).start()
        zero_ref[...] = jnp.zeros_like(zero_ref)
        for e in range(N_EXPERTS):
            pl.when(nonempty(e))(lambda e=e: zero_copy(e).start())
        for e in range(N_EXPERTS):
            pl.when(nonempty(e))(lambda e=e: zero_copy(e).wait())

        def zero_unused_tile(j, _):
            start = pl.multiple_of(j * (TM_EXP * ROW_CHUNKS), TM_EXP * ROW_CHUNKS)
            cp = pltpu.make_async_copy(zero_ref, xd_ref.at[pl.ds(start, TM_EXP * ROW_CHUNKS)], sem_zero)
            cp.start()
            cp.wait()
            return 0

        n_tiles = xd_ref.shape[0] // (TM_EXP * ROW_CHUNKS)
        lax.fori_loop(pend_ref[N_EXPERTS - 1] // TM_EXP, n_tiles, zero_unused_tile, 0)

    idx_copy(i, slot).wait()

    @pl.when(i + 1 < n)
    def _():
        idx_copy(i + 1, 1 - slot).start()

    for r0 in range(0, tm, ISSUE_BATCH):
        batch = [(r, k, idx_ref[slot, k * tm + r]) for r in range(r0, r0 + ISSUE_BATCH) for k in range(2)]
        for r, k, d in batch:
            pltpu.make_async_copy(_token_rows(hp_ref, r), _token_rows(xd_ref, d), sem_rows).start(priority=k)
    for k in range(2):
        _row_copy_wait(hp_ref, xd_ref, sem_rows, tm)


def _dispatch(pend, dest, hp, n_rows):
    T = hp.shape[0] // ROW_CHUNKS
    tm = TM_ROWS
    grid_spec = pltpu.PrefetchScalarGridSpec(
        num_scalar_prefetch=1,
        grid=(T // tm,),
        in_specs=[pl.BlockSpec(memory_space=pl.ANY),
                  pl.BlockSpec((tm * ROW_CHUNKS, LANES), lambda i, pend: (i, 0))],
        out_specs=pl.BlockSpec(memory_space=pl.ANY),
        scratch_shapes=[pltpu.SMEM((2, 2 * tm), I32), pltpu.VMEM((TM_EXP * ROW_CHUNKS, LANES), U32),
                        pltpu.SemaphoreType.DMA((2,)), pltpu.SemaphoreType.DMA, pltpu.SemaphoreType.DMA],
    )
    return pl.pallas_call(
        _dispatch_kernel,
        grid_spec=grid_spec,
        out_shape=jax.ShapeDtypeStruct((n_rows * ROW_CHUNKS, LANES), U32),
        compiler_params=_cparams(1),
        name="dispatch",
    )(pend, dest, hp)


def _experts_kernel(te_ref, nv_ref, xd_ref, wg_ref, wu_ref, wdn_ref, yd_ref, wgu_ref, wd_ref):
    i = pl.program_id(0)
    sub_rows = TM_EXP // EXP_SPLIT
    used = i < nv_ref[0]
    new_expert = jnp.logical_or(i == 0, te_ref[i] != te_ref[jnp.maximum(i - 1, 0)])

    @pl.when(jnp.logical_and(used, new_expert))
    def _():
        wgu_ref[:, :D_EXPERT] = wg_ref[0].astype(BF16)
        wgu_ref[:, D_EXPERT:] = wu_ref[0].astype(BF16)
        wd_ref[...] = wdn_ref[0].astype(BF16)

    def rows_pass(part):
        lo, hi = _unpack_rows(_load_token_rows(xd_ref, part * sub_rows, sub_rows))
        lo, hi = lo.astype(BF16), hi.astype(BF16)
        yield 1
        gu = (jnp.dot(lo, wgu_ref[:HALF, :], preferred_element_type=F32)
              + jnp.dot(hi, wgu_ref[HALF:, :], preferred_element_type=F32))
        yield 2
        gate = gu[:, :D_EXPERT]
        up = gu[:, D_EXPERT:]
        hmid = (gate * jax.nn.sigmoid(gate) * up).astype(BF16)
        yield 3
        y = jnp.dot(hmid, wd_ref[...], preferred_element_type=F32)
        yield 4
        _store_token_rows(yd_ref, part * sub_rows, _pack_rows(y))

    @pl.when(jnp.logical_not(used))
    def _():
        yd_ref[...] = jnp.zeros_like(yd_ref)

    @pl.when(used)
    def _():
        groups = [rows_pass(part) for part in range(EXP_SPLIT)]
        live = True
        step = 0
        while live:
            live = False
            for part, group in enumerate(groups):
                if step >= part and next(group, None) is not None:
                    live = True
            step += 1


def _experts(tile_expert, n_valid, xd, w_gate, w_up, w_down):
    n_tiles = tile_expert.shape[0]
    tm = TM_EXP
    tile = lambda i, te, nv: (jnp.minimum(i, nv[0] - 1), 0)
    weight = lambda i, te, nv: (te[jnp.minimum(i, nv[0] - 1)], 0, 0)
    grid_spec = pltpu.PrefetchScalarGridSpec(
        num_scalar_prefetch=2,
        grid=(n_tiles,),
        in_specs=[pl.BlockSpec((tm * ROW_CHUNKS, LANES), tile),
                  pl.BlockSpec((1, D_MODEL, D_EXPERT), weight),
                  pl.BlockSpec((1, D_MODEL, D_EXPERT), weight),
                  pl.BlockSpec((1, D_EXPERT, D_MODEL), weight)],
        out_specs=pl.BlockSpec((tm * ROW_CHUNKS, LANES), lambda i, te, nv: (i, 0)),
        scratch_shapes=[pltpu.VMEM((D_MODEL, 2 * D_EXPERT), BF16), pltpu.VMEM((D_EXPERT, D_MODEL), BF16)],
    )
    return pl.pallas_call(
        _experts_kernel,
        grid_spec=grid_spec,
        out_shape=jax.ShapeDtypeStruct((n_tiles * tm * ROW_CHUNKS, LANES), U32),
        compiler_params=_cparams(1),
        name="experts",
    )(tile_expert, n_valid, xd, w_gate, w_up, w_down)


def _combine_kernel(dest_ref, yd_ref, x2_ref, ri_ref, g_ref, o_ref, idx_ref, rows_ref, sem_idx, sem_rows):
    tm = TM_ROWS
    i = pl.program_id(0)
    n = pl.num_programs(0)
    slot = i % 2

    def fetch(step, s):
        cp = pltpu.make_async_copy(dest_ref.at[step], idx_ref.at[s], sem_idx)
        cp.start()
        cp.wait()
        for r0 in range(0, tm, ISSUE_BATCH):
            batch = [(r, k, idx_ref[s, k * tm + r]) for r in range(r0, r0 + ISSUE_BATCH) for k in range(2)]
            for r, k, d in batch:
                pltpu.make_async_copy(_token_rows(yd_ref, d), _token_rows(rows_ref.at[s, k], r),
                                      sem_rows.at[s]).start(priority=k)

    @pl.when(i == 0)
    def _():
        fetch(0, 0)

    @pl.when(i + 1 < n)
    def _():
        fetch(i + 1, 1 - slot)

    for k in range(2):
        _row_copy_wait(yd_ref, rows_ref.at[slot, k], sem_rows.at[slot], tm)

    ri = ri_ref[...]
    x2 = x2_ref[...]
    y_lo = jnp.zeros((tm, HALF), F32)
    y_hi = jnp.zeros((tm, HALF), F32)
    for k in range(2):
        lo, hi = _unpack_rows(_load_token_rows(rows_ref.at[slot, k], 0, tm))
        wk = ri[:, 2 + k:3 + k]
        y_lo = y_lo + wk * lo
        y_hi = y_hi + wk * hi
    x3 = x2 + jnp.concatenate([y_lo, y_hi], axis=1)
    o_ref[...] = _rms(x3, g_ref[...])


def _combine(dest, yd, x2, rinfo, g):
    T = x2.shape[0]
    tm = TM_ROWS
    return pl.pallas_call(
        _combine_kernel,
        grid=(T // tm,),
        in_specs=[pl.BlockSpec(memory_space=pl.ANY),
                  pl.BlockSpec(memory_space=pl.ANY),
                  pl.BlockSpec((tm, D_MODEL), lambda i: (i, 0)),
                  pl.BlockSpec((tm, LANES), lambda i: (i, 0)),
                  pl.BlockSpec((1, D_MODEL), lambda i: (0, 0))],
        out_specs=pl.BlockSpec((tm, D_MODEL), lambda i: (i, 0)),
        out_shape=jax.ShapeDtypeStruct((T, D_MODEL), F32),
        scratch_shapes=[pltpu.SMEM((2, 2 * tm), I32), pltpu.VMEM((2, 2, tm * ROW_CHUNKS, LANES), U32),
                        pltpu.SemaphoreType.DMA, pltpu.SemaphoreType.DMA((2,))],
        compiler_params=_cparams(1),
        name="combine",
    )(dest, yd, x2, rinfo, g.reshape(1, D_MODEL))


def _routing_tables(rinfo, counts_row, T):
    tm = TM_EXP
    counts = counts_row[RLANE_E:RLANE_E + N_EXPERTS].astype(I32)
    padded = (counts + tm - 1) // tm * tm
    pend = jnp.cumsum(padded)
    pstart = pend - padded
    e = rinfo[:, 0:2].astype(I32)
    rank = rinfo[:, 4:6].astype(I32)
    ids = jnp.arange(N_EXPERTS, dtype=I32)
    dest = rank + jnp.sum(jnp.where(e[..., None] == ids, pstart, 0), axis=-1)
    n_tiles = (2 * T) // tm + N_EXPERTS
    tile_start = jnp.arange(n_tiles, dtype=I32) * tm
    tile_expert = jnp.minimum(jnp.sum((tile_start[:, None] >= pend[None, :]).astype(I32), axis=1),
                              N_EXPERTS - 1)
    n_valid = (pend[-1:] // tm).astype(I32)
    ntt = T // TM_ROWS
    dest_tiles = dest.reshape(ntt, TM_ROWS, 2).transpose(0, 2, 1).reshape(ntt, 2 * TM_ROWS)
    return dest_tiles, tile_expert, n_valid, pend.astype(I32), n_tiles


def kernel(x, mem, ln_mix_g, w_in, sb_out_g, conv_w, conv_b, conv_ln_g, conv_ln_b, w_out, ln_mem_x_g,
           ln_mem_g, w_xq, w_xkv, w_xo, ln_ffn_g, w_group, b_group, w_er, b_er, w_gate, w_up, w_down,
           ln_final_g):
    B, S, D = x.shape
    T = B * S
    assert ln_mix_g.shape[0] == 1, "single-layer pipeline: the output norm is fused into the combine kernel"
    assert D == D_MODEL and S % TS_CONV == 0 and S % TM_PROJ == 0 and T % TM_ROWS == 0
    l = 0
    qt4, k5, vt5, u = _inproj(x, ln_mix_g[l], w_in[l].astype(BF16))
    sb = _sb_attention(qt4, k5, vt5, sb_out_g[l])
    cv = _conv_branch(u, conv_w[l], conv_b[l], conv_ln_g[l], conv_ln_b[l])
    kt, vx = _memkv(mem, ln_mem_g[l], w_xkv[l].astype(BF16))

    pad = LANES - RLANE_E - N_EXPERTS
    w_router = jnp.concatenate([w_group[l], w_er[l].reshape(D, N_EXPERTS), jnp.zeros((D, pad), F32)], axis=1)
    b_router = jnp.concatenate([b_group[l], b_er[l].reshape(N_EXPERTS), jnp.zeros((pad,), F32)])
    x2, hp, rinfo, cnt = _post(x, sb, cv, w_out[l].astype(BF16), ln_mem_x_g[l], w_xq[l].astype(BF16),
                               kt, vx, w_xo[l].astype(BF16), ln_ffn_g[l],
                               w_router.astype(BF16), b_router.reshape(1, LANES))

    dest_tiles, tile_expert, n_valid, pend, n_tiles = _routing_tables(rinfo, cnt[0], T)
    xd = _dispatch(pend, dest_tiles, hp, n_tiles * TM_EXP)
    yd = _experts(tile_expert, n_valid, xd, w_gate[l], w_up[l], w_down[l])
    return _combine(dest_tiles, yd, x2.reshape(T, D), rinfo, ln_final_g).reshape(B, S, D)
```

```python
import functools

import jax
import jax.numpy as jnp
from jax import lax
from jax.experimental import pallas as pl
from jax.experimental.pallas import tpu as pltpu

F32 = jnp.float32
BF16 = jnp.bfloat16
I32 = jnp.int32
U32 = jnp.uint32

EPS = 1e-6

D_MODEL = 1024
N_SB_HEADS = 8
SB_HEAD_DIM = 64
D_SB = N_SB_HEADS * SB_HEAD_DIM
D_CONV = D_MODEL - D_SB
CONV_WIDTH = 31
N_XHEADS = 4
XHEAD_DIM = D_MODEL // N_XHEADS
N_GROUPS = 4
EXPERTS_PER_GROUP = 8
N_EXPERTS = N_GROUPS * EXPERTS_PER_GROUP
D_EXPERT = 512

LANES = 128
SUBLANES = 8
VMEM_LIMIT = 56 * 1024 * 1024

TM_PROJ = 512
SB_BLK = 256
SB_GROUP = SB_BLK // SUBLANES
TS_CONV = 512
CONV_HALO = 32
CONV_CHUNK = 32
CONV_L = TS_CONV // SUBLANES
CONV_EXT_ROWS = (CONV_L + CONV_HALO) * SUBLANES
CONV_PITCH_IN = 100
CONV_PITCH_OUT = 68
TM_POST = 512
POST_SPLIT = 2
TM_ROWS = 512
TM_EXP = 512
EXP_SPLIT = 2
ISSUE_BATCH = 8
DISPATCH_TILES = 2
COMBINE_CHUNKS = 8
HALF = D_MODEL // 2
ROW_CHUNKS = HALF // LANES
RLANE_E = 4
NEG_BIG = -1e30
LOG2E = 1.4426950408889634
LOGIT_CAP = 64.0


def _cparams(n_axes):
    return pltpu.CompilerParams(dimension_semantics=("arbitrary",) * n_axes,
                                vmem_limit_bytes=VMEM_LIMIT)


def _rms(x, g):
    ms = jnp.mean(x * x, axis=-1, keepdims=True)
    return x * lax.rsqrt(ms + EPS) * g


def _pack_rows(y):
    lo = pltpu.bitcast(y[:, :HALF].astype(BF16).astype(F32), U32)
    hi = pltpu.bitcast(y[:, HALF:].astype(BF16).astype(F32), U32)
    return (lo >> 16) | (hi & jnp.uint32(0xFFFF0000))


def _unpack_rows(w):
    lo = pltpu.bitcast(w << 16, F32)
    hi = pltpu.bitcast(w & jnp.uint32(0xFFFF0000), F32)
    return lo, hi


def _store_token_rows(ref, first, words):
    rows = words.shape[0]
    for c in range(ROW_CHUNKS):
        ref[pl.ds(first * ROW_CHUNKS + c, rows, stride=ROW_CHUNKS), :] = words[:, c * LANES:(c + 1) * LANES]


def _load_token_rows(ref, first, rows):
    return jnp.concatenate([ref[pl.ds(first * ROW_CHUNKS + c, rows, stride=ROW_CHUNKS), :]
                            for c in range(ROW_CHUNKS)], axis=1)


def _inproj_kernel(x_ref, g_ref, w_ref, qt_ref, k_ref, vt_ref, u_ref, perm_ref):
    blk = SB_BLK
    npair = N_SB_HEADS // 2

    @pl.when(jnp.logical_and(pl.program_id(0) == 0, pl.program_id(1) == 0))
    def _():
        p = lax.broadcasted_iota(I32, (blk, blk), 0)
        c = lax.broadcasted_iota(I32, (blk, blk), 1)
        perm_ref[...] = jnp.where(c == (p & (SUBLANES - 1)) * SB_GROUP + (p >> 3), 1.0, 0.0).astype(BF16)

    h = _rms(x_ref[0], g_ref[...]).astype(BF16)
    p = jnp.dot(h, w_ref[...], preferred_element_type=F32)
    u_ref[0] = p[:, 3 * D_SB:].astype(BF16)
    q = p[:, :D_SB] * (SB_HEAD_DIM ** -0.5 * LOG2E)
    for j in range(npair):
        qt_ref[0, j] = q[:, j * LANES:(j + 1) * LANES].T.astype(BF16)
    for nb in range(TM_PROJ // blk):
        rows = slice(nb * blk, (nb + 1) * blk)
        kv = p[rows, D_SB:3 * D_SB].astype(BF16)
        kvp = jnp.dot(perm_ref[...], kv, preferred_element_type=F32)
        for j in range(npair):
            k_ref[0, j, nb] = kvp[:, j * LANES:(j + 1) * LANES].astype(BF16)
            vt = kvp[:, D_SB + j * LANES:D_SB + (j + 1) * LANES].T.astype(BF16)
            vt_ref[0, 2 * j, nb] = vt[:SB_HEAD_DIM]
            vt_ref[0, 2 * j + 1, nb] = vt[SB_HEAD_DIM:]


def _inproj(x, g, w):
    B, S, _ = x.shape
    n_out = w.shape[1]
    tm = TM_PROJ
    blk = SB_BLK
    nb = tm // blk
    npair = N_SB_HEADS // 2
    return pl.pallas_call(
        _inproj_kernel,
        grid=(B, S // tm),
        in_specs=[pl.BlockSpec((1, tm, D_MODEL), lambda b, i: (b, i, 0)),
                  pl.BlockSpec((1, D_MODEL), lambda b, i: (0, 0)),
                  pl.BlockSpec((D_MODEL, n_out), lambda b, i: (0, 0))],
        out_specs=[pl.BlockSpec((1, npair, LANES, tm), lambda b, i: (b, 0, 0, i)),
                   pl.BlockSpec((1, npair, nb, blk, LANES), lambda b, i: (b, 0, i, 0, 0)),
                   pl.BlockSpec((1, N_SB_HEADS, nb, SB_HEAD_DIM, blk), lambda b, i: (b, 0, i, 0, 0)),
                   pl.BlockSpec((1, tm, 2 * D_CONV), lambda b, i: (b, i, 0))],
        out_shape=[jax.ShapeDtypeStruct((B, npair, LANES, S), BF16),
                   jax.ShapeDtypeStruct((B, npair, S // blk, blk, LANES), BF16),
                   jax.ShapeDtypeStruct((B, N_SB_HEADS, S // blk, SB_HEAD_DIM, blk), BF16),
                   jax.ShapeDtypeStruct((B, S, 2 * D_CONV), BF16)],
        scratch_shapes=[pltpu.VMEM((blk, blk), BF16)],
        compiler_params=_cparams(2),
        name="inproj",
    )(x, g.reshape(1, D_MODEL), w)


def _sb_kernel(qt_ref, k_ref, vt_ref, g_ref, o_ref, ot_ref, carry_ref, qmt_ref, bias_ref,
               z0, z1, z2, e0, e1, e2, w0, w1, w2, at0, at1, at2):
    qi = pl.program_id(1)
    blk = SB_BLK
    n_items = N_SB_HEADS * (qi + 1)

    @pl.when(jnp.logical_and(pl.program_id(0) == 0, qi == 0))
    def _():
        rowp = lax.broadcasted_iota(I32, (blk, blk), 0)
        colc = lax.broadcasted_iota(I32, (blk, blk), 1)
        key_local = (rowp & (SUBLANES - 1)) * SB_GROUP + (rowp >> 3)
        bias_ref[0] = jnp.zeros((blk, blk), F32)
        bias_ref[1] = jnp.where(key_local < colc, 0.0, NEG_BIG)
        bias_ref[2] = jnp.full((blk, blk), NEG_BIG, F32)

    ot_ref[...] = jnp.zeros_like(ot_ref)
    carry_ref[...] = jnp.ones_like(carry_ref)
    at1[...] = jnp.zeros_like(at1)
    at2[...] = jnp.zeros_like(at2)
    e2[...] = jnp.zeros_like(e2)
    w2[...] = jnp.zeros_like(w2)
    rowi = lax.broadcasted_iota(I32, (LANES, blk), 0)
    for j in range(N_SB_HEADS // 2):
        qs = qt_ref[0, j]
        zero = jnp.zeros_like(qs)
        qmt_ref[2 * j] = jnp.where(rowi < SB_HEAD_DIM, qs, zero)
        qmt_ref[2 * j + 1] = jnp.where(rowi >= SB_HEAD_DIM, qs, zero)
    sub = lax.broadcasted_iota(I32, (SUBLANES, blk), 0)

    def item(m):
        mc = jnp.minimum(m, n_items - 1)
        return qi - (mc >> 3), mc & (N_SB_HEADS - 1)

    def z_phase(m, zw):
        kb, head = item(m)
        zw[...] = jnp.dot(k_ref[0, head >> 1, kb], qmt_ref[head], preferred_element_type=F32)

    def e1_phase(m, zr, er, wr):
        kb, head = item(m)
        bidx = jnp.where(m >= n_items, 2, jnp.where(kb == qi, 1, 0))
        pex = jnp.ones((SUBLANES, blk), F32)
        for i in reversed(range(SB_GROUP)):
            rows = slice(SUBLANES * i, SUBLANES * (i + 1))
            z = zr[rows, :] + bias_ref[bidx, rows, :]
            t = jnp.exp2(jnp.minimum(z, LOGIT_CAP))
            keep = 1.0 / (1.0 + t)
            er[rows, :] = (t * keep) * pex
            pex = pex * keep
        incl = pex
        for s in (1, 2, 4):
            shifted = pltpu.roll(incl, SUBLANES - s, 0)
            incl = incl * jnp.where(sub + s < SUBLANES, shifted, 1.0)
        later = jnp.where(sub + 1 < SUBLANES, pltpu.roll(incl, SUBLANES - 1, 0), 1.0)
        carry = carry_ref[head]
        w = carry * later
        carry_ref[head] = carry * jnp.broadcast_to(incl[0:1], (SUBLANES, blk))
        wr[...] = jnp.concatenate([w, w], axis=0)

    def e2_phase(er, wr, atw):
        scale = wr[...]
        for i in range(0, SB_GROUP, 2):
            rows = slice(SUBLANES * i, SUBLANES * (i + 2))
            atw[rows, :] = (er[rows, :] * scale).astype(BF16)

    def av_phase(m, atr):
        kb, head = item(jnp.maximum(m - 2, 0))
        ot_ref[head] = ot_ref[head] + jnp.dot(vt_ref[0, head, kb], atr[...], preferred_element_type=F32)

    zs, es, ws, ats = (z0, z1, z2), (e0, e1, e2), (w0, w1, w2), (at0, at1, at2)
    z_phase(0, z0)
    z_phase(1, z1)

    def body(t, _):
        for u in range(3):
            m = 3 * t + u
            z_phase(m + 2, zs[(u + 2) % 3])
            e1_phase(m, zs[u], es[u], ws[u])
            e2_phase(es[(u + 2) % 3], ws[(u + 2) % 3], ats[(u + 2) % 3])
            av_phase(m, ats[(u + 1) % 3])
        return 0

    lax.fori_loop(0, (n_items + 4) // 3, body, 0)
    parts = [ot_ref[2 * j:2 * j + 2].reshape(LANES, blk).T for j in range(N_SB_HEADS // 2)]
    o_ref[0] = _rms(jnp.concatenate(parts, axis=1), g_ref[...]).astype(BF16)


def _sb_attention(qt4, k5, vt5, g):
    B, npair, _, S = qt4.shape
    nkb = S // SB_BLK
    blk = SB_BLK
    return pl.pallas_call(
        _sb_kernel,
        grid=(B, nkb),
        in_specs=[pl.BlockSpec((1, npair, LANES, blk), lambda b, i: (b, 0, 0, i)),
                  pl.BlockSpec((1, npair, nkb, blk, LANES), lambda b, i: (b, 0, 0, 0, 0)),
                  pl.BlockSpec((1, N_SB_HEADS, nkb, SB_HEAD_DIM, blk), lambda b, i: (b, 0, 0, 0, 0)),
                  pl.BlockSpec((1, D_SB), lambda b, i: (0, 0))],
        out_specs=pl.BlockSpec((1, blk, D_SB), lambda b, i: (b, i, 0)),
        out_shape=jax.ShapeDtypeStruct((B, S, D_SB), BF16),
        scratch_shapes=[pltpu.VMEM((N_SB_HEADS, SB_HEAD_DIM, blk), F32),
                        pltpu.VMEM((N_SB_HEADS, SUBLANES, blk), F32),
                        pltpu.VMEM((N_SB_HEADS, LANES, blk), BF16),
                        pltpu.VMEM((3, blk, blk), F32),
                        ] + [pltpu.VMEM((blk, blk), F32)] * 6
                        + [pltpu.VMEM((2 * SUBLANES, blk), F32)] * 3
                        + [pltpu.VMEM((blk, blk), BF16)] * 3,
        compiler_params=_cparams(2),
        name="sb_attn",
    )(qt4, k5, vt5, g.reshape(1, D_SB))


def _conv_kernel(u_ref, uh_ref, cw_ref, cb_ref, lg_ref, lb_ref, o_ref, cwb_ref, gs_ref, gext_ref, yp_ref, os_ref):
    step = pl.program_id(1)
    ts = TS_CONV
    nslab = D_CONV // LANES
    seg = CONV_L + CONV_HALO

    @pl.when(jnp.logical_and(pl.program_id(0) == 0, step == 0))
    def _():
        for w in range(CONV_WIDTH):
            cwb_ref[w] = jnp.broadcast_to(cw_ref[w:w + 1, :], (SUBLANES, D_CONV))

    def glu(u):
        a = u[:, :D_CONV].astype(F32)
        gate = u[:, D_CONV:].astype(F32)
        return a * jax.nn.sigmoid(gate)

    gh = jnp.where(step > 0, glu(uh_ref[0]), 0.0)
    g = jnp.concatenate([gh, glu(u_ref[0])], axis=0)
    for r in range(SUBLANES):
        for l in range(nslab):
            gs_ref[l, r * CONV_PITCH_IN:r * CONV_PITCH_IN + seg, :] = (
                g[r * CONV_L:r * CONV_L + seg, l * LANES:(l + 1) * LANES])
    for e in range(seg):
        for l in range(nslab):
            gext_ref[e * SUBLANES:(e + 1) * SUBLANES, l * LANES:(l + 1) * LANES] = (
                gs_ref[l, pl.ds(e, SUBLANES, stride=CONV_PITCH_IN), :])
    first = CONV_HALO - (CONV_WIDTH - 1)
    rows = CONV_CHUNK

    def chunk(c, _):
        r0 = pl.multiple_of(c * rows, rows)
        acc = jnp.zeros((rows // SUBLANES, SUBLANES, D_CONV), F32)
        for w in range(CONV_WIDTH):
            start = pl.multiple_of(r0 + (first + w) * SUBLANES, SUBLANES)
            gw = gext_ref[pl.ds(start, rows), :].reshape(rows // SUBLANES, SUBLANES, D_CONV)
            acc = acc + gw * cwb_ref[w]
        yp_ref[pl.ds(r0, rows), :] = acc.reshape(rows, D_CONV)
        return 0

    lax.fori_loop(0, ts // rows, chunk, 0)
    y = yp_ref[...] + cb_ref[...]
    mu = jnp.mean(y, axis=-1, keepdims=True)
    d = y - mu
    var = jnp.mean(d * d, axis=-1, keepdims=True)
    yn = d * lax.rsqrt(var + EPS) * lg_ref[...] + lb_ref[...]
    act = yn * jax.nn.sigmoid(yn)
    for i in range(CONV_L):
        for l in range(nslab):
            os_ref[l, pl.ds(i, SUBLANES, stride=CONV_PITCH_OUT), :] = (
                act[i * SUBLANES:(i + 1) * SUBLANES, l * LANES:(l + 1) * LANES])
    for r in range(SUBLANES):
        for l in range(nslab):
            o_ref[0, r * CONV_L:(r + 1) * CONV_L, l * LANES:(l + 1) * LANES] = (
                os_ref[l, r * CONV_PITCH_OUT:r * CONV_PITCH_OUT + CONV_L, :].astype(BF16))


def _conv_branch(u3, conv_w, conv_b, ln_g, ln_b):
    B, S, _ = u3.shape
    ts = TS_CONV
    halo_per_step = ts // CONV_HALO
    vec = lambda a: a.reshape(1, D_CONV)
    cw = jnp.concatenate([conv_w, jnp.zeros((CONV_HALO - CONV_WIDTH, D_CONV), F32)], axis=0)
    const = lambda rows: pl.BlockSpec((rows, D_CONV), lambda b, i: (0, 0))
    return pl.pallas_call(
        _conv_kernel,
        grid=(B, S // ts),
        in_specs=[pl.BlockSpec((1, ts, 2 * D_CONV), lambda b, i: (b, i, 0)),
                  pl.BlockSpec((1, CONV_HALO, 2 * D_CONV),
                               lambda b, i: (b, jnp.maximum(i * halo_per_step - 1, 0), 0)),
                  const(CONV_HALO), const(1), const(1), const(1)],
        out_specs=pl.BlockSpec((1, ts, D_CONV), lambda b, i: (b, i, 0)),
        out_shape=jax.ShapeDtypeStruct((B, S, D_CONV), BF16),
        scratch_shapes=[pltpu.VMEM((CONV_HALO, SUBLANES, D_CONV), F32),
                        pltpu.VMEM((D_CONV // LANES, SUBLANES * CONV_PITCH_IN, LANES), F32),
                        pltpu.VMEM((CONV_EXT_ROWS, D_CONV), F32),
                        pltpu.VMEM((ts, D_CONV), F32),
                        pltpu.VMEM((D_CONV // LANES, SUBLANES * CONV_PITCH_OUT, LANES), F32)],
        compiler_params=_cparams(2),
        name="conv",
    )(u3, u3, cw, vec(conv_b), vec(ln_g), vec(ln_b))


def _memkv_kernel(m_ref, g_ref, w_ref, kt_ref, v_ref):
    mn = _rms(m_ref[0], g_ref[...]).astype(BF16)
    kv = jnp.dot(mn, w_ref[...], preferred_element_type=F32)
    kt_ref[0] = kv[:, :D_MODEL].T.astype(BF16)
    v_ref[0] = kv[:, D_MODEL:].astype(BF16)


def _memkv(mem, g, w):
    B, M, _ = mem.shape
    return pl.pallas_call(
        _memkv_kernel,
        grid=(B,),
        in_specs=[pl.BlockSpec((1, M, D_MODEL), lambda b: (b, 0, 0)),
                  pl.BlockSpec((1, D_MODEL), lambda b: (0, 0)),
                  pl.BlockSpec((D_MODEL, 2 * D_MODEL), lambda b: (0, 0))],
        out_specs=[pl.BlockSpec((1, D_MODEL, M), lambda b: (b, 0, 0)),
                   pl.BlockSpec((1, M, D_MODEL), lambda b: (b, 0, 0))],
        out_shape=[jax.ShapeDtypeStruct((B, D_MODEL, M), BF16),
                   jax.ShapeDtypeStruct((B, M, D_MODEL), BF16)],
        compiler_params=_cparams(1),
        name="memkv",
    )(mem, g.reshape(1, D_MODEL), w)


def _post_kernel(*refs):
    cnt_ref = refs[-1]

    @pl.when(jnp.logical_and(pl.program_id(0) == 0, pl.program_id(1) == 0))
    def _():
        cnt_ref[...] = jnp.zeros_like(cnt_ref)

    groups = [_post_rows(part, *refs) for part in range(POST_SPLIT)]
    live = True
    step = 0
    while live:
        live = False
        for part, group in enumerate(groups):
            if step >= part and next(group, None) is not None:
                live = True
        step += 1


def _post_rows(part, x_ref, sb_ref, cv_ref, wo_ref, gx_ref, wq_ref, kt_ref, vx_ref, wxo_ref, gf_ref,
               wr_ref, br_ref, x2_ref, hp_ref, ri_ref, cnt_ref):
    tm = TM_POST // POST_SPLIT
    rows = slice(part * tm, (part + 1) * tm)

    x1 = (x_ref[0, rows, :]
          + jnp.dot(sb_ref[0, rows, :], wo_ref[:D_SB, :], preferred_element_type=F32)
          + jnp.dot(cv_ref[0, rows, :], wo_ref[D_SB:, :], preferred_element_type=F32))
    yield 1

    hq = _rms(x1, gx_ref[...]).astype(BF16)
    yield 2
    q = (jnp.dot(hq, wq_ref[...], preferred_element_type=F32) * (XHEAD_DIM ** -0.5)).astype(BF16)
    yield 3
    outs = []
    for h in range(N_XHEADS):
        hs = slice(h * XHEAD_DIM, (h + 1) * XHEAD_DIM)
        s = jnp.dot(q[:, hs], kt_ref[0, hs, :], preferred_element_type=F32)
        s = s - jnp.max(s, axis=-1, keepdims=True)
        p = jnp.exp(s)
        p = p / jnp.sum(p, axis=-1, keepdims=True)
        outs.append(jnp.dot(p.astype(BF16), vx_ref[0, :, hs], preferred_element_type=F32))
    o = jnp.concatenate(outs, axis=1).astype(BF16)
    yield 4
    x2 = x1 + jnp.dot(o, wxo_ref[...], preferred_element_type=F32)
    x2_ref[0, rows, :] = x2
    yield 5

    hn = _rms(x2, gf_ref[...])
    _store_token_rows(hp_ref, part * tm, _pack_rows(hn))
    yield 6
    logits = jnp.dot(hn.astype(BF16), wr_ref[...], preferred_element_type=F32) + br_ref[...]
    yield 7
    lane = lax.broadcasted_iota(I32, (tm, LANES), 1)
    big = jnp.int32(LANES)
    ninf = jnp.float32(-jnp.inf)

    def first_argmax(vals):
        m = jnp.max(vals, axis=-1, keepdims=True)
        idx = jnp.min(jnp.where(vals == m, lane, big), axis=-1, keepdims=True)
        return m, idx

    is_group = lane < N_GROUPS
    gl = jnp.where(is_group, logits, ninf)
    gmax, gidx = first_argmax(gl)
    gsum = jnp.sum(jnp.where(is_group, jnp.exp(logits - gmax), 0.0), axis=-1, keepdims=True)
    g_w = 1.0 / gsum
    lo = RLANE_E + gidx * EXPERTS_PER_GROUP
    in_group = jnp.logical_and(lane >= lo, lane < lo + EXPERTS_PER_GROUP)
    el = jnp.where(in_group, logits, ninf)
    v1, i1 = first_argmax(el)
    el2 = jnp.where(lane == i1, ninf, el)
    v2, i2 = first_argmax(el2)
    t = jnp.exp(v2 - v1)
    w1 = g_w / (1.0 + t)
    w2 = g_w * t / (1.0 + t)
    yield 8

    hit1 = lane == i1
    hit2 = lane == i2
    oh = jnp.where(jnp.logical_or(hit1, hit2), 1.0, 0.0)
    rr = lax.broadcasted_iota(I32, (tm, tm), 0)
    cc = lax.broadcasted_iota(I32, (tm, tm), 1)
    ltri = jnp.where(cc < rr, 1.0, 0.0).astype(BF16)
    before = jnp.dot(ltri, oh.astype(BF16), preferred_element_type=F32) + cnt_ref[0:1, :]
    rank1 = jnp.sum(jnp.where(hit1, before, 0.0), axis=-1, keepdims=True)
    rank2 = jnp.sum(jnp.where(hit2, before, 0.0), axis=-1, keepdims=True)
    cnt_ref[...] = cnt_ref[...] + jnp.sum(oh, axis=0, keepdims=True)

    e1 = (i1 - RLANE_E).astype(F32)
    e2 = (i2 - RLANE_E).astype(F32)
    cols = (e1, e2, w1, w2, rank1, rank2)
    info = jnp.zeros((tm, LANES), F32)
    for c, val in enumerate(cols):
        info = jnp.where(lane == c, val, info)
    ri_ref[rows, :] = info


def _post(x, sb, cv, w_out, gx, w_xq, kt, vx, w_xo, gf, w_router, b_router):
    B, S, _ = x.shape
    T = B * S
    tm = TM_POST
    spt = S // tm
    tok = lambda width: pl.BlockSpec((1, tm, width), lambda b, i: (b, i, 0))
    flat = lambda width: pl.BlockSpec((tm, width), lambda b, i: (b * spt + i, 0))
    const = lambda r, c: pl.BlockSpec((r, c), lambda b, i: (0, 0))
    M = vx.shape[1]
    return pl.pallas_call(
        _post_kernel,
        grid=(B, spt),
        in_specs=[tok(D_MODEL), tok(D_SB), tok(D_CONV),
                  const(D_MODEL, D_MODEL), const(1, D_MODEL), const(D_MODEL, D_MODEL),
                  pl.BlockSpec((1, D_MODEL, M), lambda b, i: (b, 0, 0)),
                  pl.BlockSpec((1, M, D_MODEL), lambda b, i: (b, 0, 0)),
                  const(D_MODEL, D_MODEL), const(1, D_MODEL),
                  const(D_MODEL, LANES), const(1, LANES)],
        out_specs=[tok(D_MODEL), pl.BlockSpec((tm * ROW_CHUNKS, LANES), lambda b, i: (b * spt + i, 0)), flat(LANES),
                   pl.BlockSpec((SUBLANES, LANES), lambda b, i: (0, 0))],
        out_shape=[jax.ShapeDtypeStruct((B, S, D_MODEL), F32),
                   jax.ShapeDtypeStruct((T * ROW_CHUNKS, LANES), U32),
                   jax.ShapeDtypeStruct((T, LANES), F32),
                   jax.ShapeDtypeStruct((SUBLANES, LANES), F32)],
        compiler_params=_cparams(2),
        name="post",
    )(x, sb, cv, w_out, gx.reshape(1, D_MODEL), w_xq, kt, vx, w_xo, gf.reshape(1, D_MODEL),
      w_router, b_router)


def _token_rows(ref, token):
    return ref.at[pl.ds(pl.multiple_of(token * ROW_CHUNKS, ROW_CHUNKS), ROW_CHUNKS)]


def _row_copy_wait(src_rows, dst_rows, sem, n):
    pltpu.make_async_copy(src_rows.at[pl.ds(0, n * ROW_CHUNKS)], dst_rows.at[pl.ds(0, n * ROW_CHUNKS)], sem).wait()


def _dispatch_kernel(pend_ref, dest_ref, hp_ref, xd_ref, idx_ref, zero_ref, sem_idx, sem_rows, sem_zero):
    tm = TM_ROWS
    i = pl.program_id(0)
    n = pl.num_programs(0)
    slot = i % 2

    def idx_copy(step, s):
        return pltpu.make_async_copy(dest_ref.at[step], idx_ref.at[s], sem_idx.at[s])

    def zero_copy(e):
        start = pl.multiple_of((pend_ref[e] - TM_EXP) * ROW_CHUNKS, TM_EXP * ROW_CHUNKS)
        return pltpu.make_async_copy(zero_ref, xd_ref.at[pl.ds(start, TM_EXP * ROW_CHUNKS)], sem_zero)

    def nonempty(e):
        return pend_ref[e] > (pend_ref[e - 1] if e else 0)

    @pl.when(i == 0)
    def _():
        idx_copy(0, 0).start()
        zero_ref[...] = jnp.zeros_like(zero_ref)
        for e in range(N_EXPERTS):
            pl.when(nonempty(e))(lambda e=e: zero_copy(e).start())
        for e in range(N_EXPERTS):
            pl.when(nonempty(e))(lambda e=e: zero_copy(e).wait())

        def zero_unused_tile(j, _):
            start = pl.multiple_of(j * (TM_EXP * ROW_CHUNKS), TM_EXP * ROW_CHUNKS)
            cp = pltpu.make_async_copy(zero_ref, xd_ref.at[pl.ds(start, TM_EXP * ROW_CHUNKS)], sem_zero)
            cp.start()
            cp.wait()
            return 0

        n_tiles = xd_ref.shape[0] // (TM_EXP * ROW_CHUNKS)
        lax.fori_loop(pend_ref[N_EXPERTS - 1] // TM_EXP, n_tiles, zero_unused_tile, 0)

    idx_copy(i, slot).wait()

    @pl.when(i + 1 < n)
    def _():
        idx_copy(i + 1, 1 - slot).start()

    for h in range(DISPATCH_TILES):
        for r0 in range(0, tm, ISSUE_BATCH):
            batch = [(r, k, idx_ref[slot, (2 * h + k) * tm + r]) for r in range(r0, r0 + ISSUE_BATCH) for k in range(2)]
            for r, k, d in batch:
                pltpu.make_async_copy(_token_rows(hp_ref, h * tm + r), _token_rows(xd_ref, d),
                                      sem_rows).start(priority=k)
    for _ in range(2 * DISPATCH_TILES):
        _row_copy_wait(hp_ref, xd_ref, sem_rows, tm)


def _dispatch(pend, dest, hp, n_rows):
    T = hp.shape[0] // ROW_CHUNKS
    tm = TM_ROWS
    grid_spec = pltpu.PrefetchScalarGridSpec(
        num_scalar_prefetch=1,
        grid=(T // (DISPATCH_TILES * tm),),
        in_specs=[pl.BlockSpec(memory_space=pl.ANY),
                  pl.BlockSpec((DISPATCH_TILES * tm * ROW_CHUNKS, LANES), lambda i, pend: (i, 0))],
        out_specs=pl.BlockSpec(memory_space=pl.ANY),
        scratch_shapes=[pltpu.SMEM((2, 2 * DISPATCH_TILES * tm), I32),
                        pltpu.VMEM((TM_EXP * ROW_CHUNKS, LANES), U32),
                        pltpu.SemaphoreType.DMA((2,)), pltpu.SemaphoreType.DMA, pltpu.SemaphoreType.DMA],
    )
    return pl.pallas_call(
        _dispatch_kernel,
        grid_spec=grid_spec,
        out_shape=jax.ShapeDtypeStruct((n_rows * ROW_CHUNKS, LANES), U32),
        compiler_params=_cparams(1),
        name="dispatch",
    )(pend, dest.reshape(-1, 2 * DISPATCH_TILES * tm), hp)


def _experts_kernel(te_ref, nv_ref, xd_ref, wg_ref, wu_ref, wdn_ref, yd_ref, wgu_ref, wd_ref):
    i = pl.program_id(0)
    sub_rows = TM_EXP // EXP_SPLIT
    used = i < nv_ref[0]
    new_expert = jnp.logical_or(i == 0, te_ref[i] != te_ref[jnp.maximum(i - 1, 0)])

    @pl.when(jnp.logical_and(used, new_expert))
    def _():
        wgu_ref[:, :D_EXPERT] = wg_ref[0].astype(BF16)
        wgu_ref[:, D_EXPERT:] = wu_ref[0].astype(BF16)
        wd_ref[...] = wdn_ref[0].astype(BF16)

    def rows_pass(part):
        lo, hi = _unpack_rows(_load_token_rows(xd_ref, part * sub_rows, sub_rows))
        lo, hi = lo.astype(BF16), hi.astype(BF16)
        yield 1
        gu = (jnp.dot(lo, wgu_ref[:HALF, :], preferred_element_type=F32)
              + jnp.dot(hi, wgu_ref[HALF:, :], preferred_element_type=F32))
        yield 2
        gate = gu[:, :D_EXPERT]
        up = gu[:, D_EXPERT:]
        hmid = (gate * jax.nn.sigmoid(gate) * up).astype(BF16)
        yield 3
        y = jnp.dot(hmid, wd_ref[...], preferred_element_type=F32)
        yield 4
        _store_token_rows(yd_ref, part * sub_rows, _pack_rows(y))

    @pl.when(jnp.logical_not(used))
    def _():
        yd_ref[...] = jnp.zeros_like(yd_ref)

    @pl.when(used)
    def _():
        groups = [rows_pass(part) for part in range(EXP_SPLIT)]
        live = True
        step = 0
        while live:
            live = False
            for part, group in enumerate(groups):
                if step >= part and next(group, None) is not None:
                    live = True
            step += 1


def _experts(tile_expert, n_valid, xd, w_gate, w_up, w_down):
    n_tiles = tile_expert.shape[0]
    tm = TM_EXP
    tile = lambda i, te, nv: (jnp.minimum(i, nv[0] - 1), 0)
    weight = lambda i, te, nv: (te[jnp.minimum(i, nv[0] - 1)], 0, 0)
    grid_spec = pltpu.PrefetchScalarGridSpec(
        num_scalar_prefetch=2,
        grid=(n_tiles,),
        in_specs=[pl.BlockSpec((tm * ROW_CHUNKS, LANES), tile),
                  pl.BlockSpec((1, D_MODEL, D_EXPERT), weight),
                  pl.BlockSpec((1, D_MODEL, D_EXPERT), weight),
                  pl.BlockSpec((1, D_EXPERT, D_MODEL), weight)],
        out_specs=pl.BlockSpec((tm * ROW_CHUNKS, LANES), lambda i, te, nv: (i, 0)),
        scratch_shapes=[pltpu.VMEM((D_MODEL, 2 * D_EXPERT), BF16), pltpu.VMEM((D_EXPERT, D_MODEL), BF16)],
    )
    return pl.pallas_call(
        _experts_kernel,
        grid_spec=grid_spec,
        out_shape=jax.ShapeDtypeStruct((n_tiles * tm * ROW_CHUNKS, LANES), U32),
        compiler_params=_cparams(1),
        name="experts",
    )(tile_expert, n_valid, xd, w_gate, w_up, w_down)


def _combine_kernel(dest_ref, yd_ref, x2_ref, ri_ref, g_ref, o_ref, idx_ref, rows_ref, sem_idx, sem_rows):
    tm = TM_ROWS
    j = pl.program_id(0)
    n = pl.num_programs(0)
    last = 2 * n - 1

    def idx_copy(tile, s):
        return pltpu.make_async_copy(dest_ref.at[jnp.minimum(tile, last)], idx_ref.at[s], sem_idx.at[s])

    def issue_rows(s, first, count):
        for r0 in range(first, first + count, ISSUE_BATCH):
            batch = [(r, k, idx_ref[s, k * tm + r]) for r in range(r0, r0 + ISSUE_BATCH) for k in range(2)]
            for r, k, d in batch:
                pltpu.make_async_copy(_token_rows(yd_ref, d), _token_rows(rows_ref.at[s, k], r),
                                      sem_rows.at[s]).start(priority=k)

    def wait_rows(s):
        for k in range(2):
            _row_copy_wait(yd_ref, rows_ref.at[s, k], sem_rows.at[s], tm)

    def consume(s, tile):
        idx_copy(tile + 1, 1 - s).wait()
        wait_rows(s)
        idx_copy(tile + 2, s).start()
        crows = tm // COMBINE_CHUNKS
        issuing = COMBINE_CHUNKS // 2
        per_chunk = tm // issuing
        for c in range(COMBINE_CHUNKS):
            rows = slice(s * tm + c * crows, s * tm + (c + 1) * crows)
            ri = ri_ref[rows, :]
            y_lo = jnp.zeros((crows, HALF), F32)
            y_hi = jnp.zeros((crows, HALF), F32)
            for k in range(2):
                lo, hi = _unpack_rows(_load_token_rows(rows_ref.at[s, k], c * crows, crows))
                wk = ri[:, 2 + k:3 + k]
                y_lo = y_lo + wk * lo
                y_hi = y_hi + wk * hi
            x3 = x2_ref[rows, :] + jnp.concatenate([y_lo, y_hi], axis=1)
            o_ref[rows, :] = _rms(x3, g_ref[...])
            if c < issuing:
                issue_rows(1 - s, c * per_chunk, per_chunk)

    @pl.when(j == 0)
    def _():
        first_idx = idx_copy(0, 0)
        first_idx.start()
        first_idx.wait()
        issue_rows(0, 0, tm)
        idx_copy(1, 1).start()

    consume(0, 2 * j)
    consume(1, 2 * j + 1)

    @pl.when(j == n - 1)
    def _():
        idx_copy(last, 1).wait()
        wait_rows(0)


def _combine(dest, yd, x2, rinfo, g):
    T = x2.shape[0]
    tm = TM_ROWS
    return pl.pallas_call(
        _combine_kernel,
        grid=(T // (2 * tm),),
        in_specs=[pl.BlockSpec(memory_space=pl.ANY),
                  pl.BlockSpec(memory_space=pl.ANY),
                  pl.BlockSpec((2 * tm, D_MODEL), lambda i: (i, 0)),
                  pl.BlockSpec((2 * tm, LANES), lambda i: (i, 0)),
                  pl.BlockSpec((1, D_MODEL), lambda i: (0, 0))],
        out_specs=pl.BlockSpec((2 * tm, D_MODEL), lambda i: (i, 0)),
        out_shape=jax.ShapeDtypeStruct((T, D_MODEL), F32),
        scratch_shapes=[pltpu.SMEM((2, 2 * tm), I32), pltpu.VMEM((2, 2, tm * ROW_CHUNKS, LANES), U32),
                        pltpu.SemaphoreType.DMA((2,)), pltpu.SemaphoreType.DMA((2,))],
        compiler_params=_cparams(1),
        name="combine",
    )(dest, yd, x2, rinfo, g.reshape(1, D_MODEL))


def _routing_tables(rinfo, counts_row, T):
    tm = TM_EXP
    counts = counts_row[RLANE_E:RLANE_E + N_EXPERTS].astype(I32)
    padded = (counts + tm - 1) // tm * tm
    pend = jnp.cumsum(padded)
    pstart = pend - padded
    e = rinfo[:, 0:2].astype(I32)
    rank = rinfo[:, 4:6].astype(I32)
    ids = jnp.arange(N_EXPERTS, dtype=I32)
    dest = rank + jnp.sum(jnp.where(e[..., None] == ids, pstart, 0), axis=-1)
    n_tiles = (2 * T) // tm + N_EXPERTS
    tile_start = jnp.arange(n_tiles, dtype=I32) * tm
    tile_expert = jnp.minimum(jnp.sum((tile_start[:, None] >= pend[None, :]).astype(I32), axis=1),
                              N_EXPERTS - 1)
    n_valid = (pend[-1:] // tm).astype(I32)
    ntt = T // TM_ROWS
    dest_tiles = dest.reshape(ntt, TM_ROWS, 2).transpose(0, 2, 1).reshape(ntt, 2 * TM_ROWS)
    return dest_tiles, tile_expert, n_valid, pend.astype(I32), n_tiles


def kernel(x, mem, ln_mix_g, w_in, sb_out_g, conv_w, conv_b, conv_ln_g, conv_ln_b, w_out, ln_mem_x_g,
           ln_mem_g, w_xq, w_xkv, w_xo, ln_ffn_g, w_group, b_group, w_er, b_er, w_gate, w_up, w_down,
           ln_final_g):
    B, S, D = x.shape
    T = B * S
    assert ln_mix_g.shape[0] == 1, "single-layer pipeline: the output norm is fused into the combine kernel"
    assert D == D_MODEL and S % TS_CONV == 0 and S % TM_PROJ == 0 and T % (2 * TM_ROWS) == 0
    l = 0
    qt4, k5, vt5, u = _inproj(x, ln_mix_g[l], w_in[l].astype(BF16))
    sb = _sb_attention(qt4, k5, vt5, sb_out_g[l])
    cv = _conv_branch(u, conv_w[l], conv_b[l], conv_ln_g[l], conv_ln_b[l])
    kt, vx = _memkv(mem, ln_mem_g[l], w_xkv[l].astype(BF16))

    pad = LANES - RLANE_E - N_EXPERTS
    w_router = jnp.concatenate([w_group[l], w_er[l].reshape(D, N_EXPERTS), jnp.zeros((D, pad), F32)], axis=1)
    b_router = jnp.concatenate([b_group[l], b_er[l].reshape(N_EXPERTS), jnp.zeros((pad,), F32)])
    x2, hp, rinfo, cnt = _post(x, sb, cv, w_out[l].astype(BF16), ln_mem_x_g[l], w_xq[l].astype(BF16),
                               kt, vx, w_xo[l].astype(BF16), ln_ffn_g[l],
                               w_router.astype(BF16), b_router.reshape(1, LANES))

    dest_tiles, tile_expert, n_valid, pend, n_tiles = _routing_tables(rinfo, cnt[0], T)
    xd = _dispatch(pend, dest_tiles, hp, n_tiles * TM_EXP)
    yd = _experts(tile_expert, n_valid, xd, w_gate[l], w_up[l], w_down[l])
    return _combine(dest_tiles, yd, x2.reshape(T, D), rinfo, ln_final_g).reshape(B, S, D)
```

```python
import functools

import jax
import numpy as np
import jax.numpy as jnp
from jax import lax
from jax.experimental import pallas as pl
from jax.experimental.pallas import tpu as pltpu

F32 = jnp.float32
BF16 = jnp.bfloat16
I32 = jnp.int32
U32 = jnp.uint32

EPS = 1e-6

D_MODEL = 1024
N_SB_HEADS = 8
SB_HEAD_DIM = 64
D_SB = N_SB_HEADS * SB_HEAD_DIM
D_CONV = D_MODEL - D_SB
CONV_WIDTH = 31
N_XHEADS = 4
XHEAD_DIM = D_MODEL // N_XHEADS
N_GROUPS = 4
EXPERTS_PER_GROUP = 8
N_EXPERTS = N_GROUPS * EXPERTS_PER_GROUP
D_EXPERT = 512

LANES = 128
SUBLANES = 8
VMEM_LIMIT = 56 * 1024 * 1024

TM_PROJ = 512
SB_BLK = 256
SB_GROUP = SB_BLK // SUBLANES
TS_CONV = 512
CONV_HALO = 32
CONV_CHUNK = 32
CONV_L = TS_CONV // SUBLANES
CONV_EXT_ROWS = (CONV_L + CONV_HALO) * SUBLANES
CONV_PITCH_IN = 100
CONV_PITCH_OUT = 68
TM_POST = 512
POST_SPLIT = 2
TM_ROWS = 512
TM_EXP = 512
EXP_SPLIT = 2
ISSUE_BATCH = 8
DISPATCH_TILES = 2
COMBINE_CHUNKS = 8
HALF = D_MODEL // 2
ROW_CHUNKS = HALF // LANES
RLANE_E = 4
NEG_BIG = -1e30
LOG2E = 1.4426950408889634
LOGIT_CAP = 64.0


def _cparams(n_axes):
    return pltpu.CompilerParams(dimension_semantics=("arbitrary",) * n_axes,
                                vmem_limit_bytes=VMEM_LIMIT)


def _rms(x, g):
    ms = jnp.mean(x * x, axis=-1, keepdims=True)
    return x * lax.rsqrt(ms + EPS) * g


def _pack_rows(y):
    lo = pltpu.bitcast(y[:, :HALF].astype(BF16).astype(F32), U32)
    hi = pltpu.bitcast(y[:, HALF:].astype(BF16).astype(F32), U32)
    return (lo >> 16) | (hi & jnp.uint32(0xFFFF0000))


def _unpack_rows(w):
    lo = pltpu.bitcast(w << 16, F32)
    hi = pltpu.bitcast(w & jnp.uint32(0xFFFF0000), F32)
    return lo, hi


def _store_token_rows(ref, first, words):
    rows = words.shape[0]
    for c in range(ROW_CHUNKS):
        ref[pl.ds(first * ROW_CHUNKS + c, rows, stride=ROW_CHUNKS), :] = words[:, c * LANES:(c + 1) * LANES]


def _load_token_rows(ref, first, rows):
    return jnp.concatenate([ref[pl.ds(first * ROW_CHUNKS + c, rows, stride=ROW_CHUNKS), :]
                            for c in range(ROW_CHUNKS)], axis=1)


def _inproj_kernel(x_ref, g_ref, w_ref, qt_ref, k_ref, vt_ref, u_ref, perm_ref):
    blk = SB_BLK
    npair = N_SB_HEADS // 2

    @pl.when(jnp.logical_and(pl.program_id(0) == 0, pl.program_id(1) == 0))
    def _():
        p = lax.broadcasted_iota(I32, (blk, blk), 0)
        c = lax.broadcasted_iota(I32, (blk, blk), 1)
        perm_ref[...] = jnp.where(c == (p & (SUBLANES - 1)) * SB_GROUP + (p >> 3), 1.0, 0.0).astype(BF16)

    h = _rms(x_ref[0], g_ref[...]).astype(BF16)
    p = jnp.dot(h, w_ref[...], preferred_element_type=F32)
    u_ref[0] = p[:, 3 * D_SB:].astype(BF16)
    q = p[:, :D_SB] * (SB_HEAD_DIM ** -0.5 * LOG2E)
    for j in range(npair):
        qt_ref[0, j] = q[:, j * LANES:(j + 1) * LANES].T.astype(BF16)
    for nb in range(TM_PROJ // blk):
        rows = slice(nb * blk, (nb + 1) * blk)
        kv = p[rows, D_SB:3 * D_SB].astype(BF16)
        kvp = jnp.dot(perm_ref[...], kv, preferred_element_type=F32)
        for j in range(npair):
            k_ref[0, j, nb] = kvp[:, j * LANES:(j + 1) * LANES].astype(BF16)
            vt = kvp[:, D_SB + j * LANES:D_SB + (j + 1) * LANES].T.astype(BF16)
            vt_ref[0, 2 * j, nb] = vt[:SB_HEAD_DIM]
            vt_ref[0, 2 * j + 1, nb] = vt[SB_HEAD_DIM:]


def _inproj(x, g, w):
    B, S, _ = x.shape
    n_out = w.shape[1]
    tm = TM_PROJ
    blk = SB_BLK
    nb = tm // blk
    npair = N_SB_HEADS // 2
    return pl.pallas_call(
        _inproj_kernel,
        grid=(B, S // tm),
        in_specs=[pl.BlockSpec((1, tm, D_MODEL), lambda b, i: (b, i, 0)),
                  pl.BlockSpec((1, D_MODEL), lambda b, i: (0, 0)),
                  pl.BlockSpec((D_MODEL, n_out), lambda b, i: (0, 0))],
        out_specs=[pl.BlockSpec((1, npair, LANES, tm), lambda b, i: (b, 0, 0, i)),
                   pl.BlockSpec((1, npair, nb, blk, LANES), lambda b, i: (b, 0, i, 0, 0)),
                   pl.BlockSpec((1, N_SB_HEADS, nb, SB_HEAD_DIM, blk), lambda b, i: (b, 0, i, 0, 0)),
                   pl.BlockSpec((1, tm, 2 * D_CONV), lambda b, i: (b, i, 0))],
        out_shape=[jax.ShapeDtypeStruct((B, npair, LANES, S), BF16),
                   jax.ShapeDtypeStruct((B, npair, S // blk, blk, LANES), BF16),
                   jax.ShapeDtypeStruct((B, N_SB_HEADS, S // blk, SB_HEAD_DIM, blk), BF16),
                   jax.ShapeDtypeStruct((B, S, 2 * D_CONV), BF16)],
        scratch_shapes=[pltpu.VMEM((blk, blk), BF16)],
        compiler_params=_cparams(2),
        name="inproj",
    )(x, g.reshape(1, D_MODEL), w)


def _sb_item_table(nqb):
    items = [(qb, kb, head, 1 if kb == qb else 0)
             for qb in range(nqb) for kb in range(qb, -1, -1) for head in range(N_SB_HEADS)]
    n_items = len(items)
    trips = -(-(n_items + 2) // 3)
    items += [items[-1][:3] + (2,)] * (3 * trips + 2 - n_items)
    return np.asarray(items, np.int32).T.copy(), n_items, trips


def _sb_kernel(tab_ref, qt_ref, k_ref, vt_ref, g_ref, o_ref, ot_ref, carry_ref, qmt_ref, bias_ref,
               z0, z1, z2, e0, e1, e2, w0, w1, w2, at0, at1, at2, *, nqb, trips):
    blk = SB_BLK

    @pl.when(pl.program_id(0) == 0)
    def _():
        rowp = lax.broadcasted_iota(I32, (blk, blk), 0)
        colc = lax.broadcasted_iota(I32, (blk, blk), 1)
        key_local = (rowp & (SUBLANES - 1)) * SB_GROUP + (rowp >> 3)
        bias_ref[0] = jnp.zeros((blk, blk), F32)
        bias_ref[1] = jnp.where(key_local < colc, 0.0, NEG_BIG)
        bias_ref[2] = jnp.full((blk, blk), NEG_BIG, F32)

    ot_ref[...] = jnp.zeros_like(ot_ref)
    carry_ref[...] = jnp.ones_like(carry_ref)
    at1[...] = jnp.zeros_like(at1)
    at2[...] = jnp.zeros_like(at2)
    e2[...] = jnp.zeros_like(e2)
    w2[...] = jnp.zeros_like(w2)
    rowi = lax.broadcasted_iota(I32, (LANES, blk), 0)
    for qb in range(nqb):
        for j in range(N_SB_HEADS // 2):
            qs = qt_ref[0, j, :, qb * blk:(qb + 1) * blk]
            zero = jnp.zeros_like(qs)
            qmt_ref[qb * N_SB_HEADS + 2 * j] = jnp.where(rowi < SB_HEAD_DIM, qs, zero)
            qmt_ref[qb * N_SB_HEADS + 2 * j + 1] = jnp.where(rowi >= SB_HEAD_DIM, qs, zero)
    sub = lax.broadcasted_iota(I32, (SUBLANES, blk), 0)

    def item(m):
        return tab_ref[0, m], tab_ref[1, m], tab_ref[2, m]

    def z_phase(m, zw):
        qb, kb, head = item(m)
        zw[...] = jnp.dot(k_ref[0, head >> 1, kb], qmt_ref[qb * N_SB_HEADS + head],
                          preferred_element_type=F32)

    def e1_phase(m, zr, er, wr):
        qb, kb, head = item(m)
        bidx = tab_ref[3, m]
        state = qb * N_SB_HEADS + head
        pex = jnp.ones((SUBLANES, blk), F32)
        for i in reversed(range(SB_GROUP)):
            rows = slice(SUBLANES * i, SUBLANES * (i + 1))
            z = zr[rows, :] + bias_ref[bidx, rows, :]
            t = jnp.exp2(jnp.minimum(z, LOGIT_CAP))
            keep = 1.0 / (1.0 + t)
            er[rows, :] = (t * keep) * pex
            pex = pex * keep
        incl = pex
        for s in (1, 2, 4):
            shifted = pltpu.roll(incl, SUBLANES - s, 0)
            incl = incl * jnp.where(sub + s < SUBLANES, shifted, 1.0)
        later = jnp.where(sub + 1 < SUBLANES, pltpu.roll(incl, SUBLANES - 1, 0), 1.0)
        carry = carry_ref[state]
        w = carry * later
        carry_ref[state] = carry * jnp.broadcast_to(incl[0:1], (SUBLANES, blk))
        wr[...] = jnp.concatenate([w, w], axis=0)

    def e2_phase(er, wr, atw):
        scale = wr[...]
        for i in range(0, SB_GROUP, 2):
            rows = slice(SUBLANES * i, SUBLANES * (i + 2))
            atw[rows, :] = (er[rows, :] * scale).astype(BF16)

    def av_phase(m, atr):
        qb, kb, head = item(jnp.maximum(m - 2, 0))
        state = qb * N_SB_HEADS + head
        ot_ref[state] = ot_ref[state] + jnp.dot(vt_ref[0, head, kb], atr[...], preferred_element_type=F32)

    zs, es, ws, ats = (z0, z1, z2), (e0, e1, e2), (w0, w1, w2), (at0, at1, at2)
    z_phase(0, z0)
    z_phase(1, z1)

    def body(t, _):
        for u in range(3):
            m = 3 * t + u
            z_phase(m + 2, zs[(u + 2) % 3])
            e1_phase(m, zs[u], es[u], ws[u])
            e2_phase(es[(u + 2) % 3], ws[(u + 2) % 3], ats[(u + 2) % 3])
            av_phase(m, ats[(u + 1) % 3])
        return 0

    lax.fori_loop(0, trips, body, 0)
    for qb in range(nqb):
        first = qb * N_SB_HEADS
        parts = [ot_ref[first + 2 * j:first + 2 * j + 2].reshape(LANES, blk).T for j in range(N_SB_HEADS // 2)]
        o_ref[0, qb * blk:(qb + 1) * blk, :] = _rms(jnp.concatenate(parts, axis=1), g_ref[...]).astype(BF16)


def _sb_attention(qt4, k5, vt5, g):
    B, npair, _, S = qt4.shape
    nkb = S // SB_BLK
    blk = SB_BLK
    table, n_items, trips = _sb_item_table(nkb)
    del n_items
    grid_spec = pltpu.PrefetchScalarGridSpec(
        num_scalar_prefetch=1,
        grid=(B,),
        in_specs=[pl.BlockSpec((1, npair, LANES, S), lambda b, tab: (b, 0, 0, 0)),
                  pl.BlockSpec((1, npair, nkb, blk, LANES), lambda b, tab: (b, 0, 0, 0, 0)),
                  pl.BlockSpec((1, N_SB_HEADS, nkb, SB_HEAD_DIM, blk), lambda b, tab: (b, 0, 0, 0, 0)),
                  pl.BlockSpec((1, D_SB), lambda b, tab: (0, 0))],
        out_specs=pl.BlockSpec((1, S, D_SB), lambda b, tab: (b, 0, 0)),
        scratch_shapes=[pltpu.VMEM((nkb * N_SB_HEADS, SB_HEAD_DIM, blk), F32),
                        pltpu.VMEM((nkb * N_SB_HEADS, SUBLANES, blk), F32),
                        pltpu.VMEM((nkb * N_SB_HEADS, LANES, blk), BF16),
                        pltpu.VMEM((3, blk, blk), F32),
                        ] + [pltpu.VMEM((blk, blk), F32)] * 6
                        + [pltpu.VMEM((2 * SUBLANES, blk), F32)] * 3
                        + [pltpu.VMEM((blk, blk), BF16)] * 3,
    )
    return pl.pallas_call(
        functools.partial(_sb_kernel, nqb=nkb, trips=trips),
        grid_spec=grid_spec,
        out_shape=jax.ShapeDtypeStruct((B, S, D_SB), BF16),
        compiler_params=_cparams(1),
        name="sb_attn",
    )(jnp.asarray(table), qt4, k5, vt5, g.reshape(1, D_SB))


def _conv_kernel(u_ref, uh_ref, cw_ref, cb_ref, lg_ref, lb_ref, o_ref, cwb_ref, gs_ref, gext_ref, yp_ref, os_ref):
    step = pl.program_id(1)
    ts = TS_CONV
    nslab = D_CONV // LANES
    seg = CONV_L + CONV_HALO

    @pl.when(jnp.logical_and(pl.program_id(0) == 0, step == 0))
    def _():
        for w in range(CONV_WIDTH):
            cwb_ref[w] = jnp.broadcast_to(cw_ref[w:w + 1, :], (SUBLANES, D_CONV))

    def glu(u):
        a = u[:, :D_CONV].astype(F32)
        gate = u[:, D_CONV:].astype(F32)
        return a * jax.nn.sigmoid(gate)

    gh = jnp.where(step > 0, glu(uh_ref[0]), 0.0)
    g = jnp.concatenate([gh, glu(u_ref[0])], axis=0)
    for r in range(SUBLANES):
        for l in range(nslab):
            gs_ref[l, r * CONV_PITCH_IN:r * CONV_PITCH_IN + seg, :] = (
                g[r * CONV_L:r * CONV_L + seg, l * LANES:(l + 1) * LANES])
    for e in range(seg):
        for l in range(nslab):
            gext_ref[e * SUBLANES:(e + 1) * SUBLANES, l * LANES:(l + 1) * LANES] = (
                gs_ref[l, pl.ds(e, SUBLANES, stride=CONV_PITCH_IN), :])
    first = CONV_HALO - (CONV_WIDTH - 1)
    rows = CONV_CHUNK

    def chunk(c, _):
        r0 = pl.multiple_of(c * rows, rows)
        acc = jnp.zeros((rows // SUBLANES, SUBLANES, D_CONV), F32)
        for w in range(CONV_WIDTH):
            start = pl.multiple_of(r0 + (first + w) * SUBLANES, SUBLANES)
            gw = gext_ref[pl.ds(start, rows), :].reshape(rows // SUBLANES, SUBLANES, D_CONV)
            acc = acc + gw * cwb_ref[w]
        yp_ref[pl.ds(r0, rows), :] = acc.reshape(rows, D_CONV)
        return 0

    lax.fori_loop(0, ts // rows, chunk, 0)
    y = yp_ref[...] + cb_ref[...]
    mu = jnp.mean(y, axis=-1, keepdims=True)
    d = y - mu
    var = jnp.mean(d * d, axis=-1, keepdims=True)
    yn = d * lax.rsqrt(var + EPS) * lg_ref[...] + lb_ref[...]
    act = yn * jax.nn.sigmoid(yn)
    for i in range(CONV_L):
        for l in range(nslab):
            os_ref[l, pl.ds(i, SUBLANES, stride=CONV_PITCH_OUT), :] = (
                act[i * SUBLANES:(i + 1) * SUBLANES, l * LANES:(l + 1) * LANES])
    for r in range(SUBLANES):
        for l in range(nslab):
            o_ref[0, r * CONV_L:(r + 1) * CONV_L, l * LANES:(l + 1) * LANES] = (
                os_ref[l, r * CONV_PITCH_OUT:r * CONV_PITCH_OUT + CONV_L, :].astype(BF16))


def _conv_branch(u3, conv_w, conv_b, ln_g, ln_b):
    B, S, _ = u3.shape
    ts = TS_CONV
    halo_per_step = ts // CONV_HALO
    vec = lambda a: a.reshape(1, D_CONV)
    cw = jnp.concatenate([conv_w, jnp.zeros((CONV_HALO - CONV_WIDTH, D_CONV), F32)], axis=0)
    const = lambda rows: pl.BlockSpec((rows, D_CONV), lambda b, i: (0, 0))
    return pl.pallas_call(
        _conv_kernel,
        grid=(B, S // ts),
        in_specs=[pl.BlockSpec((1, ts, 2 * D_CONV), lambda b, i: (b, i, 0)),
                  pl.BlockSpec((1, CONV_HALO, 2 * D_CONV),
                               lambda b, i: (b, jnp.maximum(i * halo_per_step - 1, 0), 0)),
                  const(CONV_HALO), const(1), const(1), const(1)],
        out_specs=pl.BlockSpec((1, ts, D_CONV), lambda b, i: (b, i, 0)),
        out_shape=jax.ShapeDtypeStruct((B, S, D_CONV), BF16),
        scratch_shapes=[pltpu.VMEM((CONV_HALO, SUBLANES, D_CONV), F32),
                        pltpu.VMEM((D_CONV // LANES, SUBLANES * CONV_PITCH_IN, LANES), F32),
                        pltpu.VMEM((CONV_EXT_ROWS, D_CONV), F32),
                        pltpu.VMEM((ts, D_CONV), F32),
                        pltpu.VMEM((D_CONV // LANES, SUBLANES * CONV_PITCH_OUT, LANES), F32)],
        compiler_params=_cparams(2),
        name="conv",
    )(u3, u3, cw, vec(conv_b), vec(ln_g), vec(ln_b))


def _memkv_kernel(m_ref, g_ref, w_ref, kt_ref, v_ref):
    mn = _rms(m_ref[0], g_ref[...]).astype(BF16)
    kv = jnp.dot(mn, w_ref[...], preferred_element_type=F32)
    kt_ref[0] = kv[:, :D_MODEL].T.astype(BF16)
    v_ref[0] = kv[:, D_MODEL:].astype(BF16)


def _memkv(mem, g, w):
    B, M, _ = mem.shape
    return pl.pallas_call(
        _memkv_kernel,
        grid=(B,),
        in_specs=[pl.BlockSpec((1, M, D_MODEL), lambda b: (b, 0, 0)),
                  pl.BlockSpec((1, D_MODEL), lambda b: (0, 0)),
                  pl.BlockSpec((D_MODEL, 2 * D_MODEL), lambda b: (0, 0))],
        out_specs=[pl.BlockSpec((1, D_MODEL, M), lambda b: (b, 0, 0)),
                   pl.BlockSpec((1, M, D_MODEL), lambda b: (b, 0, 0))],
        out_shape=[jax.ShapeDtypeStruct((B, D_MODEL, M), BF16),
                   jax.ShapeDtypeStruct((B, M, D_MODEL), BF16)],
        compiler_params=_cparams(1),
        name="memkv",
    )(mem, g.reshape(1, D_MODEL), w)


def _post_kernel(*refs):
    cnt_ref = refs[-1]

    @pl.when(jnp.logical_and(pl.program_id(0) == 0, pl.program_id(1) == 0))
    def _():
        cnt_ref[...] = jnp.zeros_like(cnt_ref)

    groups = [_post_rows(part, *refs) for part in range(POST_SPLIT)]
    live = True
    step = 0
    while live:
        live = False
        for part, group in enumerate(groups):
            if step >= part and next(group, None) is not None:
                live = True
        step += 1


def _post_rows(part, x_ref, sb_ref, cv_ref, wo_ref, gx_ref, wq_ref, kt_ref, vx_ref, wxo_ref, gf_ref,
               wr_ref, br_ref, x2_ref, hp_ref, ri_ref, cnt_ref):
    tm = TM_POST // POST_SPLIT
    rows = slice(part * tm, (part + 1) * tm)

    x1 = (x_ref[0, rows, :]
          + jnp.dot(sb_ref[0, rows, :], wo_ref[:D_SB, :], preferred_element_type=F32)
          + jnp.dot(cv_ref[0, rows, :], wo_ref[D_SB:, :], preferred_element_type=F32))
    yield 1

    hq = _rms(x1, gx_ref[...]).astype(BF16)
    yield 2
    q = (jnp.dot(hq, wq_ref[...], preferred_element_type=F32) * (XHEAD_DIM ** -0.5)).astype(BF16)
    yield 3
    outs = []
    for h in range(N_XHEADS):
        hs = slice(h * XHEAD_DIM, (h + 1) * XHEAD_DIM)
        s = jnp.dot(q[:, hs], kt_ref[0, hs, :], preferred_element_type=F32)
        s = s - jnp.max(s, axis=-1, keepdims=True)
        p = jnp.exp(s)
        p = p / jnp.sum(p, axis=-1, keepdims=True)
        outs.append(jnp.dot(p.astype(BF16), vx_ref[0, :, hs], preferred_element_type=F32))
    o = jnp.concatenate(outs, axis=1).astype(BF16)
    yield 4
    x2 = x1 + jnp.dot(o, wxo_ref[...], preferred_element_type=F32)
    x2_ref[0, rows, :] = x2
    yield 5

    hn = _rms(x2, gf_ref[...])
    _store_token_rows(hp_ref, part * tm, _pack_rows(hn))
    yield 6
    logits = jnp.dot(hn.astype(BF16), wr_ref[...], preferred_element_type=F32) + br_ref[...]
    yield 7
    lane = lax.broadcasted_iota(I32, (tm, LANES), 1)
    big = jnp.int32(LANES)
    ninf = jnp.float32(-jnp.inf)

    def first_argmax(vals):
        m = jnp.max(vals, axis=-1, keepdims=True)
        idx = jnp.min(jnp.where(vals == m, lane, big), axis=-1, keepdims=True)
        return m, idx

    is_group = lane < N_GROUPS
    gl = jnp.where(is_group, logits, ninf)
    gmax, gidx = first_argmax(gl)
    gsum = jnp.sum(jnp.where(is_group, jnp.exp(logits - gmax), 0.0), axis=-1, keepdims=True)
    g_w = 1.0 / gsum
    lo = RLANE_E + gidx * EXPERTS_PER_GROUP
    in_group = jnp.logical_and(lane >= lo, lane < lo + EXPERTS_PER_GROUP)
    el = jnp.where(in_group, logits, ninf)
    v1, i1 = first_argmax(el)
    el2 = jnp.where(lane == i1, ninf, el)
    v2, i2 = first_argmax(el2)
    t = jnp.exp(v2 - v1)
    w1 = g_w / (1.0 + t)
    w2 = g_w * t / (1.0 + t)
    yield 8

    hit1 = lane == i1
    hit2 = lane == i2
    oh = jnp.where(jnp.logical_or(hit1, hit2), 1.0, 0.0)
    rr = lax.broadcasted_iota(I32, (tm, tm), 0)
    cc = lax.broadcasted_iota(I32, (tm, tm), 1)
    ltri = jnp.where(cc < rr, 1.0, 0.0).astype(BF16)
    before = jnp.dot(ltri, oh.astype(BF16), preferred_element_type=F32) + cnt_ref[0:1, :]
    rank1 = jnp.sum(jnp.where(hit1, before, 0.0), axis=-1, keepdims=True)
    rank2 = jnp.sum(jnp.where(hit2, before, 0.0), axis=-1, keepdims=True)
    cnt_ref[...] = cnt_ref[...] + jnp.sum(oh, axis=0, keepdims=True)

    e1 = (i1 - RLANE_E).astype(F32)
    e2 = (i2 - RLANE_E).astype(F32)
    cols = (e1, e2, w1, w2, rank1, rank2)
    info = jnp.zeros((tm, LANES), F32)
    for c, val in enumerate(cols):
        info = jnp.where(lane == c, val, info)
    ri_ref[rows, :] = info


def _post(x, sb, cv, w_out, gx, w_xq, kt, vx, w_xo, gf, w_router, b_router):
    B, S, _ = x.shape
    T = B * S
    tm = TM_POST
    spt = S // tm
    tok = lambda width: pl.BlockSpec((1, tm, width), lambda b, i: (b, i, 0))
    flat = lambda width: pl.BlockSpec((tm, width), lambda b, i: (b * spt + i, 0))
    const = lambda r, c: pl.BlockSpec((r, c), lambda b, i: (0, 0))
    M = vx.shape[1]
    return pl.pallas_call(
        _post_kernel,
        grid=(B, spt),
        in_specs=[tok(D_MODEL), tok(D_SB), tok(D_CONV),
                  const(D_MODEL, D_MODEL), const(1, D_MODEL), const(D_MODEL, D_MODEL),
                  pl.BlockSpec((1, D_MODEL, M), lambda b, i: (b, 0, 0)),
                  pl.BlockSpec((1, M, D_MODEL), lambda b, i: (b, 0, 0)),
                  const(D_MODEL, D_MODEL), const(1, D_MODEL),
                  const(D_MODEL, LANES), const(1, LANES)],
        out_specs=[tok(D_MODEL), pl.BlockSpec((tm * ROW_CHUNKS, LANES), lambda b, i: (b * spt + i, 0)), flat(LANES),
                   pl.BlockSpec((SUBLANES, LANES), lambda b, i: (0, 0))],
        out_shape=[jax.ShapeDtypeStruct((B, S, D_MODEL), F32),
                   jax.ShapeDtypeStruct((T * ROW_CHUNKS, LANES), U32),
                   jax.ShapeDtypeStruct((T, LANES), F32),
                   jax.ShapeDtypeStruct((SUBLANES, LANES), F32)],
        compiler_params=_cparams(2),
        name="post",
    )(x, sb, cv, w_out, gx.reshape(1, D_MODEL), w_xq, kt, vx, w_xo, gf.reshape(1, D_MODEL),
      w_router, b_router)


def _token_rows(ref, token):
    return ref.at[pl.ds(pl.multiple_of(token * ROW_CHUNKS, ROW_CHUNKS), ROW_CHUNKS)]


def _row_copy_wait(src_rows, dst_rows, sem, n):
    pltpu.make_async_copy(src_rows.at[pl.ds(0, n * ROW_CHUNKS)], dst_rows.at[pl.ds(0, n * ROW_CHUNKS)], sem).wait()


def _dispatch_kernel(pend_ref, dest_ref, hp_ref, xd_ref, idx_ref, zero_ref, sem_idx, sem_rows, sem_zero):
    tm = TM_ROWS
    i = pl.program_id(0)
    n = pl.num_programs(0)
    slot = i % 2

    def idx_copy(step, s):
        return pltpu.make_async_copy(dest_ref.at[step], idx_ref.at[s], sem_idx.at[s])

    def zero_copy(e):
        start = pl.multiple_of((pend_ref[e] - TM_EXP) * ROW_CHUNKS, TM_EXP * ROW_CHUNKS)
        return pltpu.make_async_copy(zero_ref, xd_ref.at[pl.ds(start, TM_EXP * ROW_CHUNKS)], sem_zero)

    def nonempty(e):
        return pend_ref[e] > (pend_ref[e - 1] if e else 0)

    @pl.when(i == 0)
    def _():
        idx_copy(0, 0).start()
        zero_ref[...] = jnp.zeros_like(zero_ref)
        for e in range(N_EXPERTS):
            pl.when(nonempty(e))(lambda e=e: zero_copy(e).start())
        for e in range(N_EXPERTS):
            pl.when(nonempty(e))(lambda e=e: zero_copy(e).wait())

        def zero_unused_tile(j, _):
            start = pl.multiple_of(j * (TM_EXP * ROW_CHUNKS), TM_EXP * ROW_CHUNKS)
            cp = pltpu.make_async_copy(zero_ref, xd_ref.at[pl.ds(start, TM_EXP * ROW_CHUNKS)], sem_zero)
            cp.start()
            cp.wait()
            return 0

        n_tiles = xd_ref.shape[0] // (TM_EXP * ROW_CHUNKS)
        lax.fori_loop(pend_ref[N_EXPERTS - 1] // TM_EXP, n_tiles, zero_unused_tile, 0)

    idx_copy(i, slot).wait()

    @pl.when(i + 1 < n)
    def _():
        idx_copy(i + 1, 1 - slot).start()

    for h in range(DISPATCH_TILES):
        for r0 in range(0, tm, ISSUE_BATCH):
            batch = [(r, k, idx_ref[slot, (2 * h + k) * tm + r]) for r in range(r0, r0 + ISSUE_BATCH) for k in range(2)]
            for r, k, d in batch:
                pltpu.make_async_copy(_token_rows(hp_ref, h * tm + r), _token_rows(xd_ref, d),
                                      sem_rows).start(priority=k)
    for _ in range(2 * DISPATCH_TILES):
        _row_copy_wait(hp_ref, xd_ref, sem_rows, tm)


def _dispatch(pend, dest, hp, n_rows):
    T = hp.shape[0] // ROW_CHUNKS
    tm = TM_ROWS
    grid_spec = pltpu.PrefetchScalarGridSpec(
        num_scalar_prefetch=1,
        grid=(T // (DISPATCH_TILES * tm),),
        in_specs=[pl.BlockSpec(memory_space=pl.ANY),
                  pl.BlockSpec((DISPATCH_TILES * tm * ROW_CHUNKS, LANES), lambda i, pend: (i, 0))],
        out_specs=pl.BlockSpec(memory_space=pl.ANY),
        scratch_shapes=[pltpu.SMEM((2, 2 * DISPATCH_TILES * tm), I32),
                        pltpu.VMEM((TM_EXP * ROW_CHUNKS, LANES), U32),
                        pltpu.SemaphoreType.DMA((2,)), pltpu.SemaphoreType.DMA, pltpu.SemaphoreType.DMA],
    )
    return pl.pallas_call(
        _dispatch_kernel,
        grid_spec=grid_spec,
        out_shape=jax.ShapeDtypeStruct((n_rows * ROW_CHUNKS, LANES), U32),
        compiler_params=_cparams(1),
        name="dispatch",
    )(pend, dest.reshape(-1, 2 * DISPATCH_TILES * tm), hp)


def _experts_kernel(te_ref, nv_ref, xd_ref, wg_ref, wu_ref, wdn_ref, yd_ref, wgu_ref, wd_ref):
    i = pl.program_id(0)
    sub_rows = TM_EXP // EXP_SPLIT
    used = i < nv_ref[0]
    new_expert = jnp.logical_or(i == 0, te_ref[i] != te_ref[jnp.maximum(i - 1, 0)])

    @pl.when(jnp.logical_and(used, new_expert))
    def _():
        wgu_ref[:, :D_EXPERT] = wg_ref[0].astype(BF16)
        wgu_ref[:, D_EXPERT:] = wu_ref[0].astype(BF16)
        wd_ref[...] = wdn_ref[0].astype(BF16)

    def rows_pass(part):
        lo, hi = _unpack_rows(_load_token_rows(xd_ref, part * sub_rows, sub_rows))
        lo, hi = lo.astype(BF16), hi.astype(BF16)
        yield 1
        gu = (jnp.dot(lo, wgu_ref[:HALF, :], preferred_element_type=F32)
              + jnp.dot(hi, wgu_ref[HALF:, :], preferred_element_type=F32))
        yield 2
        gate = gu[:, :D_EXPERT]
        up = gu[:, D_EXPERT:]
        hmid = (gate * jax.nn.sigmoid(gate) * up).astype(BF16)
        yield 3
        y = jnp.dot(hmid, wd_ref[...], preferred_element_type=F32)
        yield 4
        _store_token_rows(yd_ref, part * sub_rows, _pack_rows(y))

    @pl.when(jnp.logical_not(used))
    def _():
        yd_ref[...] = jnp.zeros_like(yd_ref)

    @pl.when(used)
    def _():
        groups = [rows_pass(part) for part in range(EXP_SPLIT)]
        live = True
        step = 0
        while live:
            live = False
            for part, group in enumerate(groups):
                if step >= part and next(group, None) is not None:
                    live = True
            step += 1


def _experts(tile_expert, n_valid, xd, w_gate, w_up, w_down):
    n_tiles = tile_expert.shape[0]
    tm = TM_EXP
    tile = lambda i, te, nv: (jnp.minimum(i, nv[0] - 1), 0)
    weight = lambda i, te, nv: (te[jnp.minimum(i, nv[0] - 1)], 0, 0)
    grid_spec = pltpu.PrefetchScalarGridSpec(
        num_scalar_prefetch=2,
        grid=(n_tiles,),
        in_specs=[pl.BlockSpec((tm * ROW_CHUNKS, LANES), tile),
                  pl.BlockSpec((1, D_MODEL, D_EXPERT), weight),
                  pl.BlockSpec((1, D_MODEL, D_EXPERT), weight),
                  pl.BlockSpec((1, D_EXPERT, D_MODEL), weight)],
        out_specs=pl.BlockSpec((tm * ROW_CHUNKS, LANES), lambda i, te, nv: (i, 0)),
        scratch_shapes=[pltpu.VMEM((D_MODEL, 2 * D_EXPERT), BF16), pltpu.VMEM((D_EXPERT, D_MODEL), BF16)],
    )
    return pl.pallas_call(
        _experts_kernel,
        grid_spec=grid_spec,
        out_shape=jax.ShapeDtypeStruct((n_tiles * tm * ROW_CHUNKS, LANES), U32),
        compiler_params=_cparams(1),
        name="experts",
    )(tile_expert, n_valid, xd, w_gate, w_up, w_down)


def _combine_kernel(dest_ref, yd_ref, x2_ref, ri_ref, g_ref, o_ref, idx_ref, rows_ref, sem_idx, sem_rows):
    tm = TM_ROWS
    j = pl.program_id(0)
    n = pl.num_programs(0)
    last = 2 * n - 1

    def idx_copy(tile, s):
        return pltpu.make_async_copy(dest_ref.at[jnp.minimum(tile, last)], idx_ref.at[s], sem_idx.at[s])

    def issue_rows(s, first, count):
        for r0 in range(first, first + count, ISSUE_BATCH):
            batch = [(r, k, idx_ref[s, k * tm + r]) for r in range(r0, r0 + ISSUE_BATCH) for k in range(2)]
            for r, k, d in batch:
                pltpu.make_async_copy(_token_rows(yd_ref, d), _token_rows(rows_ref.at[s, k], r),
                                      sem_rows.at[s]).start(priority=k)

    def wait_rows(s):
        for k in range(2):
            _row_copy_wait(yd_ref, rows_ref.at[s, k], sem_rows.at[s], tm)

    def consume(s, tile):
        idx_copy(tile + 1, 1 - s).wait()
        wait_rows(s)
        idx_copy(tile + 2, s).start()
        crows = tm // COMBINE_CHUNKS
        issuing = COMBINE_CHUNKS // 2
        per_chunk = tm // issuing
        for c in range(COMBINE_CHUNKS):
            rows = slice(s * tm + c * crows, s * tm + (c + 1) * crows)
            ri = ri_ref[rows, :]
            y_lo = jnp.zeros((crows, HALF), F32)
            y_hi = jnp.zeros((crows, HALF), F32)
            for k in range(2):
                lo, hi = _unpack_rows(_load_token_rows(rows_ref.at[s, k], c * crows, crows))
                wk = ri[:, 2 + k:3 + k]
                y_lo = y_lo + wk * lo
                y_hi = y_hi + wk * hi
            x3 = x2_ref[rows, :] + jnp.concatenate([y_lo, y_hi], axis=1)
            o_ref[rows, :] = _rms(x3, g_ref[...])
            if c < issuing:
                issue_rows(1 - s, c * per_chunk, per_chunk)

    @pl.when(j == 0)
    def _():
        first_idx = idx_copy(0, 0)
        first_idx.start()
        first_idx.wait()
        issue_rows(0, 0, tm)
        idx_copy(1, 1).start()

    consume(0, 2 * j)
    consume(1, 2 * j + 1)

    @pl.when(j == n - 1)
    def _():
        idx_copy(last, 1).wait()
        wait_rows(0)


def _combine(dest, yd, x2, rinfo, g):
    T = x2.shape[0]
    tm = TM_ROWS
    return pl.pallas_call(
        _combine_kernel,
        grid=(T // (2 * tm),),
        in_specs=[pl.BlockSpec(memory_space=pl.ANY),
                  pl.BlockSpec(memory_space=pl.ANY),
                  pl.BlockSpec((2 * tm, D_MODEL), lambda i: (i, 0)),
                  pl.BlockSpec((2 * tm, LANES), lambda i: (i, 0)),
                  pl.BlockSpec((1, D_MODEL), lambda i: (0, 0))],
        out_specs=pl.BlockSpec((2 * tm, D_MODEL), lambda i: (i, 0)),
        out_shape=jax.ShapeDtypeStruct((T, D_MODEL), F32),
        scratch_shapes=[pltpu.SMEM((2, 2 * tm), I32), pltpu.VMEM((2, 2, tm * ROW_CHUNKS, LANES), U32),
                        pltpu.SemaphoreType.DMA((2,)), pltpu.SemaphoreType.DMA((2,))],
        compiler_params=_cparams(1),
        name="combine",
    )(dest, yd, x2, rinfo, g.reshape(1, D_MODEL))


def _routing_tables(rinfo, counts_row, T):
    tm = TM_EXP
    counts = counts_row[RLANE_E:RLANE_E + N_EXPERTS].astype(I32)
    padded = (counts + tm - 1) // tm * tm
    pend = jnp.cumsum(padded)
    pstart = pend - padded
    e = rinfo[:, 0:2].astype(I32)
    rank = rinfo[:, 4:6].astype(I32)
    ids = jnp.arange(N_EXPERTS, dtype=I32)
    dest = rank + jnp.sum(jnp.where(e[..., None] == ids, pstart, 0), axis=-1)
    n_tiles = (2 * T) // tm + N_EXPERTS
    tile_start = jnp.arange(n_tiles, dtype=I32) * tm
    tile_expert = jnp.minimum(jnp.sum((tile_start[:, None] >= pend[None, :]).astype(I32), axis=1),
                              N_EXPERTS - 1)
    n_valid = (pend[-1:] // tm).astype(I32)
    ntt = T // TM_ROWS
    dest_tiles = dest.reshape(ntt, TM_ROWS, 2).transpose(0, 2, 1).reshape(ntt, 2 * TM_ROWS)
    return dest_tiles, tile_expert, n_valid, pend.astype(I32), n_tiles


def kernel(x, mem, ln_mix_g, w_in, sb_out_g, conv_w, conv_b, conv_ln_g, conv_ln_b, w_out, ln_mem_x_g,
           ln_mem_g, w_xq, w_xkv, w_xo, ln_ffn_g, w_group, b_group, w_er, b_er, w_gate, w_up, w_down,
           ln_final_g):
    B, S, D = x.shape
    T = B * S
    assert ln_mix_g.shape[0] == 1, "single-layer pipeline: the output norm is fused into the combine kernel"
    assert D == D_MODEL and S % TS_CONV == 0 and S % TM_PROJ == 0 and T % (2 * TM_ROWS) == 0
    l = 0
    qt4, k5, vt5, u = _inproj(x, ln_mix_g[l], w_in[l].astype(BF16))
    sb = _sb_attention(qt4, k5, vt5, sb_out_g[l])
    cv = _conv_branch(u, conv_w[l], conv_b[l], conv_ln_g[l], conv_ln_b[l])
    kt, vx = _memkv(mem, ln_mem_g[l], w_xkv[l].astype(BF16))

    pad = LANES - RLANE_E - N_EXPERTS
    w_router = jnp.concatenate([w_group[l], w_er[l].reshape(D, N_EXPERTS), jnp.zeros((D, pad), F32)], axis=1)
    b_router = jnp.concatenate([b_group[l], b_er[l].reshape(N_EXPERTS), jnp.zeros((pad,), F32)])
    x2, hp, rinfo, cnt = _post(x, sb, cv, w_out[l].astype(BF16), ln_mem_x_g[l], w_xq[l].astype(BF16),
                               kt, vx, w_xo[l].astype(BF16), ln_ffn_g[l],
                               w_router.astype(BF16), b_router.reshape(1, LANES))

    dest_tiles, tile_expert, n_valid, pend, n_tiles = _routing_tables(rinfo, cnt[0], T)
    xd = _dispatch(pend, dest_tiles, hp, n_tiles * TM_EXP)
    yd = _experts(tile_expert, n_valid, xd, w_gate[l], w_up[l], w_down[l])
    return _combine(dest_tiles, yd, x2.reshape(T, D), rinfo, ln_final_g).reshape(B, S, D)
```

```python
import functools

import jax
import numpy as np
import jax.numpy as jnp
from jax import lax
from jax.experimental import pallas as pl
from jax.experimental.pallas import tpu as pltpu

F32 = jnp.float32
BF16 = jnp.bfloat16
I32 = jnp.int32
U32 = jnp.uint32

EPS = 1e-6

D_MODEL = 1024
N_SB_HEADS = 8
SB_HEAD_DIM = 64
D_SB = N_SB_HEADS * SB_HEAD_DIM
D_CONV = D_MODEL - D_SB
CONV_WIDTH = 31
N_XHEADS = 4
XHEAD_DIM = D_MODEL // N_XHEADS
N_GROUPS = 4
EXPERTS_PER_GROUP = 8
N_EXPERTS = N_GROUPS * EXPERTS_PER_GROUP
D_EXPERT = 512

LANES = 128
SUBLANES = 8
VMEM_LIMIT = 56 * 1024 * 1024

TM_PROJ = 512
SB_BLK = 256
SB_GROUP = SB_BLK // SUBLANES
SB_BODY_ITEMS = 6
TS_CONV = 512
CONV_HALO = 32
CONV_CHUNK = 32
CONV_L = TS_CONV // SUBLANES
CONV_EXT_ROWS = (CONV_L + CONV_HALO) * SUBLANES
CONV_PITCH_IN = 100
CONV_PITCH_OUT = 68
TM_POST = 512
POST_SPLIT = 2
TM_ROWS = 512
TM_EXP = 512
EXP_SPLIT = 2
ISSUE_BATCH = 8
DISPATCH_TILES = 2
COMBINE_CHUNKS = 8
HALF = D_MODEL // 2
ROW_CHUNKS = HALF // LANES
RLANE_E = 4
NEG_BIG = -1e30
LOG2E = 1.4426950408889634
LOGIT_CAP = 64.0


def _cparams(n_axes):
    return pltpu.CompilerParams(dimension_semantics=("arbitrary",) * n_axes,
                                vmem_limit_bytes=VMEM_LIMIT)


def _rms(x, g):
    ms = jnp.mean(x * x, axis=-1, keepdims=True)
    return x * lax.rsqrt(ms + EPS) * g


def _pack_rows(y):
    lo = pltpu.bitcast(y[:, :HALF].astype(BF16).astype(F32), U32)
    hi = pltpu.bitcast(y[:, HALF:].astype(BF16).astype(F32), U32)
    return (lo >> 16) | (hi & jnp.uint32(0xFFFF0000))


def _unpack_rows(w):
    lo = pltpu.bitcast(w << 16, F32)
    hi = pltpu.bitcast(w & jnp.uint32(0xFFFF0000), F32)
    return lo, hi


def _store_token_rows(ref, first, words):
    rows = words.shape[0]
    for c in range(ROW_CHUNKS):
        ref[pl.ds(first * ROW_CHUNKS + c, rows, stride=ROW_CHUNKS), :] = words[:, c * LANES:(c + 1) * LANES]


def _load_token_rows(ref, first, rows):
    return jnp.concatenate([ref[pl.ds(first * ROW_CHUNKS + c, rows, stride=ROW_CHUNKS), :]
                            for c in range(ROW_CHUNKS)], axis=1)


def _inproj_kernel(x_ref, g_ref, w_ref, qt_ref, k_ref, vt_ref, u_ref, perm_ref):
    blk = SB_BLK
    npair = N_SB_HEADS // 2

    @pl.when(jnp.logical_and(pl.program_id(0) == 0, pl.program_id(1) == 0))
    def _():
        p = lax.broadcasted_iota(I32, (blk, blk), 0)
        c = lax.broadcasted_iota(I32, (blk, blk), 1)
        perm_ref[...] = jnp.where(c == (p & (SUBLANES - 1)) * SB_GROUP + (p >> 3), 1.0, 0.0).astype(BF16)

    h = _rms(x_ref[0], g_ref[...]).astype(BF16)
    p = jnp.dot(h, w_ref[...], preferred_element_type=F32)
    u_ref[0] = p[:, 3 * D_SB:].astype(BF16)
    q = p[:, :D_SB] * (SB_HEAD_DIM ** -0.5 * LOG2E)
    for j in range(npair):
        qt_ref[0, j] = q[:, j * LANES:(j + 1) * LANES].T.astype(BF16)
    for nb in range(TM_PROJ // blk):
        rows = slice(nb * blk, (nb + 1) * blk)
        kv = p[rows, D_SB:3 * D_SB].astype(BF16)
        kvp = jnp.dot(perm_ref[...], kv, preferred_element_type=F32)
        for j in range(npair):
            k_ref[0, j, nb] = kvp[:, j * LANES:(j + 1) * LANES].astype(BF16)
            vt = kvp[:, D_SB + j * LANES:D_SB + (j + 1) * LANES].T.astype(BF16)
            vt_ref[0, 2 * j, nb] = vt[:SB_HEAD_DIM]
            vt_ref[0, 2 * j + 1, nb] = vt[SB_HEAD_DIM:]


def _inproj(x, g, w):
    B, S, _ = x.shape
    n_out = w.shape[1]
    tm = TM_PROJ
    blk = SB_BLK
    nb = tm // blk
    npair = N_SB_HEADS // 2
    return pl.pallas_call(
        _inproj_kernel,
        grid=(B, S // tm),
        in_specs=[pl.BlockSpec((1, tm, D_MODEL), lambda b, i: (b, i, 0)),
                  pl.BlockSpec((1, D_MODEL), lambda b, i: (0, 0)),
                  pl.BlockSpec((D_MODEL, n_out), lambda b, i: (0, 0))],
        out_specs=[pl.BlockSpec((1, npair, LANES, tm), lambda b, i: (b, 0, 0, i)),
                   pl.BlockSpec((1, npair, nb, blk, LANES), lambda b, i: (b, 0, i, 0, 0)),
                   pl.BlockSpec((1, N_SB_HEADS, nb, SB_HEAD_DIM, blk), lambda b, i: (b, 0, i, 0, 0)),
                   pl.BlockSpec((1, tm, 2 * D_CONV), lambda b, i: (b, i, 0))],
        out_shape=[jax.ShapeDtypeStruct((B, npair, LANES, S), BF16),
                   jax.ShapeDtypeStruct((B, npair, S // blk, blk, LANES), BF16),
                   jax.ShapeDtypeStruct((B, N_SB_HEADS, S // blk, SB_HEAD_DIM, blk), BF16),
                   jax.ShapeDtypeStruct((B, S, 2 * D_CONV), BF16)],
        scratch_shapes=[pltpu.VMEM((blk, blk), BF16)],
        compiler_params=_cparams(2),
        name="inproj",
    )(x, g.reshape(1, D_MODEL), w)


def _sb_item_table(nqb):
    items = [(qb, kb, head, 1 if kb == qb else 0)
             for qb in range(nqb) for kb in range(qb, -1, -1) for head in range(N_SB_HEADS)]
    n_items = len(items)
    trips = -(-(n_items + 2) // SB_BODY_ITEMS)
    items += [items[-1][:3] + (2,)] * (SB_BODY_ITEMS * trips + 2 - n_items)
    return np.asarray(items, np.int32).T.copy(), n_items, trips


def _sb_kernel(tab_ref, qt_ref, k_ref, vt_ref, g_ref, o_ref, ot_ref, carry_ref, qmt_ref, bias_ref,
               z0, z1, z2, e0, e1, e2, w0, w1, w2, at0, at1, at2, *, nqb, trips):
    blk = SB_BLK

    @pl.when(pl.program_id(0) == 0)
    def _():
        rowp = lax.broadcasted_iota(I32, (blk, blk), 0)
        colc = lax.broadcasted_iota(I32, (blk, blk), 1)
        key_local = (rowp & (SUBLANES - 1)) * SB_GROUP + (rowp >> 3)
        bias_ref[0] = jnp.zeros((blk, blk), F32)
        bias_ref[1] = jnp.where(key_local < colc, 0.0, NEG_BIG)
        bias_ref[2] = jnp.full((blk, blk), NEG_BIG, F32)

    ot_ref[...] = jnp.zeros_like(ot_ref)
    carry_ref[...] = jnp.ones_like(carry_ref)
    at1[...] = jnp.zeros_like(at1)
    at2[...] = jnp.zeros_like(at2)
    e2[...] = jnp.zeros_like(e2)
    w2[...] = jnp.zeros_like(w2)
    rowi = lax.broadcasted_iota(I32, (LANES, blk), 0)
    for qb in range(nqb):
        for j in range(N_SB_HEADS // 2):
            qs = qt_ref[0, j, :, qb * blk:(qb + 1) * blk]
            zero = jnp.zeros_like(qs)
            qmt_ref[qb * N_SB_HEADS + 2 * j] = jnp.where(rowi < SB_HEAD_DIM, qs, zero)
            qmt_ref[qb * N_SB_HEADS + 2 * j + 1] = jnp.where(rowi >= SB_HEAD_DIM, qs, zero)
    sub = lax.broadcasted_iota(I32, (SUBLANES, blk), 0)

    def item(m):
        return tab_ref[0, m], tab_ref[1, m], tab_ref[2, m]

    def z_phase(m, zw):
        qb, kb, head = item(m)
        zw[...] = jnp.dot(k_ref[0, head >> 1, kb], qmt_ref[qb * N_SB_HEADS + head],
                          preferred_element_type=F32)

    def e1_phase(m, zr, er, wr):
        qb, kb, head = item(m)
        bidx = tab_ref[3, m]
        state = qb * N_SB_HEADS + head
        pex = jnp.ones((SUBLANES, blk), F32)
        for i in reversed(range(SB_GROUP)):
            rows = slice(SUBLANES * i, SUBLANES * (i + 1))
            z = zr[rows, :] + bias_ref[bidx, rows, :]
            t = jnp.exp2(jnp.minimum(z, LOGIT_CAP))
            keep = 1.0 / (1.0 + t)
            er[rows, :] = (t * keep) * pex
            pex = pex * keep
        incl = pex
        for s in (1, 2, 4):
            shifted = pltpu.roll(incl, SUBLANES - s, 0)
            incl = incl * jnp.where(sub + s < SUBLANES, shifted, 1.0)
        later = jnp.where(sub + 1 < SUBLANES, pltpu.roll(incl, SUBLANES - 1, 0), 1.0)
        carry = carry_ref[state]
        w = carry * later
        carry_ref[state] = carry * jnp.broadcast_to(incl[0:1], (SUBLANES, blk))
        wr[...] = jnp.concatenate([w, w], axis=0)

    def e2_phase(er, wr, atw):
        scale = wr[...]
        for i in range(0, SB_GROUP, 2):
            rows = slice(SUBLANES * i, SUBLANES * (i + 2))
            atw[rows, :] = (er[rows, :] * scale).astype(BF16)

    def av_phase(m, atr):
        qb, kb, head = item(jnp.maximum(m - 2, 0))
        state = qb * N_SB_HEADS + head
        ot_ref[state] = ot_ref[state] + jnp.dot(vt_ref[0, head, kb], atr[...], preferred_element_type=F32)

    zs, es, ws, ats = (z0, z1, z2), (e0, e1, e2), (w0, w1, w2), (at0, at1, at2)
    z_phase(0, z0)
    z_phase(1, z1)

    def body(t, _):
        for u in range(SB_BODY_ITEMS):
            m = SB_BODY_ITEMS * t + u
            z_phase(m + 2, zs[(u + 2) % 3])
            e1_phase(m, zs[u % 3], es[u % 3], ws[u % 3])
            e2_phase(es[(u + 2) % 3], ws[(u + 2) % 3], ats[(u + 2) % 3])
            av_phase(m, ats[(u + 1) % 3])
        return 0

    lax.fori_loop(0, trips, body, 0)
    for qb in range(nqb):
        first = qb * N_SB_HEADS
        parts = [ot_ref[first + 2 * j:first + 2 * j + 2].reshape(LANES, blk).T for j in range(N_SB_HEADS // 2)]
        o_ref[0, qb * blk:(qb + 1) * blk, :] = _rms(jnp.concatenate(parts, axis=1), g_ref[...]).astype(BF16)


def _sb_attention(qt4, k5, vt5, g):
    B, npair, _, S = qt4.shape
    nkb = S // SB_BLK
    blk = SB_BLK
    table, n_items, trips = _sb_item_table(nkb)
    del n_items
    grid_spec = pltpu.PrefetchScalarGridSpec(
        num_scalar_prefetch=1,
        grid=(B,),
        in_specs=[pl.BlockSpec((1, npair, LANES, S), lambda b, tab: (b, 0, 0, 0)),
                  pl.BlockSpec((1, npair, nkb, blk, LANES), lambda b, tab: (b, 0, 0, 0, 0)),
                  pl.BlockSpec((1, N_SB_HEADS, nkb, SB_HEAD_DIM, blk), lambda b, tab: (b, 0, 0, 0, 0)),
                  pl.BlockSpec((1, D_SB), lambda b, tab: (0, 0))],
        out_specs=pl.BlockSpec((1, S, D_SB), lambda b, tab: (b, 0, 0)),
        scratch_shapes=[pltpu.VMEM((nkb * N_SB_HEADS, SB_HEAD_DIM, blk), F32),
                        pltpu.VMEM((nkb * N_SB_HEADS, SUBLANES, blk), F32),
                        pltpu.VMEM((nkb * N_SB_HEADS, LANES, blk), BF16),
                        pltpu.VMEM((3, blk, blk), F32),
                        ] + [pltpu.VMEM((blk, blk), F32)] * 6
                        + [pltpu.VMEM((2 * SUBLANES, blk), F32)] * 3
                        + [pltpu.VMEM((blk, blk), BF16)] * 3,
    )
    return pl.pallas_call(
        functools.partial(_sb_kernel, nqb=nkb, trips=trips),
        grid_spec=grid_spec,
        out_shape=jax.ShapeDtypeStruct((B, S, D_SB), BF16),
        compiler_params=_cparams(1),
        name="sb_attn",
    )(jnp.asarray(table), qt4, k5, vt5, g.reshape(1, D_SB))


def _conv_kernel(u_ref, uh_ref, cw_ref, cb_ref, lg_ref, lb_ref, o_ref, cwb_ref, gs_ref, gext_ref, yp_ref, os_ref):
    step = pl.program_id(1)
    ts = TS_CONV
    nslab = D_CONV // LANES
    seg = CONV_L + CONV_HALO

    @pl.when(jnp.logical_and(pl.program_id(0) == 0, step == 0))
    def _():
        for w in range(CONV_WIDTH):
            cwb_ref[w] = jnp.broadcast_to(cw_ref[w:w + 1, :], (SUBLANES, D_CONV))

    def glu(u):
        a = u[:, :D_CONV].astype(F32)
        gate = u[:, D_CONV:].astype(F32)
        return a * jax.nn.sigmoid(gate)

    gh = jnp.where(step > 0, glu(uh_ref[0]), 0.0)
    g = jnp.concatenate([gh, glu(u_ref[0])], axis=0)
    for r in range(SUBLANES):
        for l in range(nslab):
            gs_ref[l, r * CONV_PITCH_IN:r * CONV_PITCH_IN + seg, :] = (
                g[r * CONV_L:r * CONV_L + seg, l * LANES:(l + 1) * LANES])
    for e in range(seg):
        for l in range(nslab):
            gext_ref[e * SUBLANES:(e + 1) * SUBLANES, l * LANES:(l + 1) * LANES] = (
                gs_ref[l, pl.ds(e, SUBLANES, stride=CONV_PITCH_IN), :])
    first = CONV_HALO - (CONV_WIDTH - 1)
    rows = CONV_CHUNK

    def chunk(c, _):
        r0 = pl.multiple_of(c * rows, rows)
        acc = jnp.zeros((rows // SUBLANES, SUBLANES, D_CONV), F32)
        for w in range(CONV_WIDTH):
            start = pl.multiple_of(r0 + (first + w) * SUBLANES, SUBLANES)
            gw = gext_ref[pl.ds(start, rows), :].reshape(rows // SUBLANES, SUBLANES, D_CONV)
            acc = acc + gw * cwb_ref[w]
        yp_ref[pl.ds(r0, rows), :] = acc.reshape(rows, D_CONV)
        return 0

    lax.fori_loop(0, ts // rows, chunk, 0)
    y = yp_ref[...] + cb_ref[...]
    mu = jnp.mean(y, axis=-1, keepdims=True)
    d = y - mu
    var = jnp.mean(d * d, axis=-1, keepdims=True)
    yn = d * lax.rsqrt(var + EPS) * lg_ref[...] + lb_ref[...]
    act = yn * jax.nn.sigmoid(yn)
    for i in range(CONV_L):
        for l in range(nslab):
            os_ref[l, pl.ds(i, SUBLANES, stride=CONV_PITCH_OUT), :] = (
                act[i * SUBLANES:(i + 1) * SUBLANES, l * LANES:(l + 1) * LANES])
    for r in range(SUBLANES):
        for l in range(nslab):
            o_ref[0, r * CONV_L:(r + 1) * CONV_L, l * LANES:(l + 1) * LANES] = (
                os_ref[l, r * CONV_PITCH_OUT:r * CONV_PITCH_OUT + CONV_L, :].astype(BF16))


def _conv_branch(u3, conv_w, conv_b, ln_g, ln_b):
    B, S, _ = u3.shape
    ts = TS_CONV
    halo_per_step = ts // CONV_HALO
    vec = lambda a: a.reshape(1, D_CONV)
    cw = jnp.concatenate([conv_w, jnp.zeros((CONV_HALO - CONV_WIDTH, D_CONV), F32)], axis=0)
    const = lambda rows: pl.BlockSpec((rows, D_CONV), lambda b, i: (0, 0))
    return pl.pallas_call(
        _conv_kernel,
        grid=(B, S // ts),
        in_specs=[pl.BlockSpec((1, ts, 2 * D_CONV), lambda b, i: (b, i, 0)),
                  pl.BlockSpec((1, CONV_HALO, 2 * D_CONV),
                               lambda b, i: (b, jnp.maximum(i * halo_per_step - 1, 0), 0)),
                  const(CONV_HALO), const(1), const(1), const(1)],
        out_specs=pl.BlockSpec((1, ts, D_CONV), lambda b, i: (b, i, 0)),
        out_shape=jax.ShapeDtypeStruct((B, S, D_CONV), BF16),
        scratch_shapes=[pltpu.VMEM((CONV_HALO, SUBLANES, D_CONV), F32),
                        pltpu.VMEM((D_CONV // LANES, SUBLANES * CONV_PITCH_IN, LANES), F32),
                        pltpu.VMEM((CONV_EXT_ROWS, D_CONV), F32),
                        pltpu.VMEM((ts, D_CONV), F32),
                        pltpu.VMEM((D_CONV // LANES, SUBLANES * CONV_PITCH_OUT, LANES), F32)],
        compiler_params=_cparams(2),
        name="conv",
    )(u3, u3, cw, vec(conv_b), vec(ln_g), vec(ln_b))


def _memkv_kernel(m_ref, g_ref, w_ref, kt_ref, v_ref):
    mn = _rms(m_ref[0], g_ref[...]).astype(BF16)
    kv = jnp.dot(mn, w_ref[...], preferred_element_type=F32)
    kt_ref[0] = kv[:, :D_MODEL].T.astype(BF16)
    v_ref[0] = kv[:, D_MODEL:].astype(BF16)


def _memkv(mem, g, w):
    B, M, _ = mem.shape
    return pl.pallas_call(
        _memkv_kernel,
        grid=(B,),
        in_specs=[pl.BlockSpec((1, M, D_MODEL), lambda b: (b, 0, 0)),
                  pl.BlockSpec((1, D_MODEL), lambda b: (0, 0)),
                  pl.BlockSpec((D_MODEL, 2 * D_MODEL), lambda b: (0, 0))],
        out_specs=[pl.BlockSpec((1, D_MODEL, M), lambda b: (b, 0, 0)),
                   pl.BlockSpec((1, M, D_MODEL), lambda b: (b, 0, 0))],
        out_shape=[jax.ShapeDtypeStruct((B, D_MODEL, M), BF16),
                   jax.ShapeDtypeStruct((B, M, D_MODEL), BF16)],
        compiler_params=_cparams(1),
        name="memkv",
    )(mem, g.reshape(1, D_MODEL), w)


def _post_kernel(*refs):
    cnt_ref = refs[-1]

    @pl.when(jnp.logical_and(pl.program_id(0) == 0, pl.program_id(1) == 0))
    def _():
        cnt_ref[...] = jnp.zeros_like(cnt_ref)

    groups = [_post_rows(part, *refs) for part in range(POST_SPLIT)]
    live = True
    step = 0
    while live:
        live = False
        for part, group in enumerate(groups):
            if step >= part and next(group, None) is not None:
                live = True
        step += 1


def _post_rows(part, x_ref, sb_ref, cv_ref, wo_ref, gx_ref, wq_ref, kt_ref, vx_ref, wxo_ref, gf_ref,
               wr_ref, br_ref, x2_ref, hp_ref, ri_ref, cnt_ref):
    tm = TM_POST // POST_SPLIT
    rows = slice(part * tm, (part + 1) * tm)

    x1 = (x_ref[0, rows, :]
          + jnp.dot(sb_ref[0, rows, :], wo_ref[:D_SB, :], preferred_element_type=F32)
          + jnp.dot(cv_ref[0, rows, :], wo_ref[D_SB:, :], preferred_element_type=F32))
    yield 1

    hq = _rms(x1, gx_ref[...]).astype(BF16)
    yield 2
    q = (jnp.dot(hq, wq_ref[...], preferred_element_type=F32) * (XHEAD_DIM ** -0.5)).astype(BF16)
    yield 3
    outs = []
    for h in range(N_XHEADS):
        hs = slice(h * XHEAD_DIM, (h + 1) * XHEAD_DIM)
        s = jnp.dot(q[:, hs], kt_ref[0, hs, :], preferred_element_type=F32)
        s = s - jnp.max(s, axis=-1, keepdims=True)
        p = jnp.exp(s)
        p = p / jnp.sum(p, axis=-1, keepdims=True)
        outs.append(jnp.dot(p.astype(BF16), vx_ref[0, :, hs], preferred_element_type=F32))
    o = jnp.concatenate(outs, axis=1).astype(BF16)
    yield 4
    x2 = x1 + jnp.dot(o, wxo_ref[...], preferred_element_type=F32)
    x2_ref[0, rows, :] = x2
    yield 5

    hn = _rms(x2, gf_ref[...])
    _store_token_rows(hp_ref, part * tm, _pack_rows(hn))
    yield 6
    logits = jnp.dot(hn.astype(BF16), wr_ref[...], preferred_element_type=F32) + br_ref[...]
    yield 7
    lane = lax.broadcasted_iota(I32, (tm, LANES), 1)
    big = jnp.int32(LANES)
    ninf = jnp.float32(-jnp.inf)

    def first_argmax(vals):
        m = jnp.max(vals, axis=-1, keepdims=True)
        idx = jnp.min(jnp.where(vals == m, lane, big), axis=-1, keepdims=True)
        return m, idx

    is_group = lane < N_GROUPS
    gl = jnp.where(is_group, logits, ninf)
    gmax, gidx = first_argmax(gl)
    gsum = jnp.sum(jnp.where(is_group, jnp.exp(logits - gmax), 0.0), axis=-1, keepdims=True)
    g_w = 1.0 / gsum
    lo = RLANE_E + gidx * EXPERTS_PER_GROUP
    in_group = jnp.logical_and(lane >= lo, lane < lo + EXPERTS_PER_GROUP)
    el = jnp.where(in_group, logits, ninf)
    v1, i1 = first_argmax(el)
    el2 = jnp.where(lane == i1, ninf, el)
    v2, i2 = first_argmax(el2)
    t = jnp.exp(v2 - v1)
    w1 = g_w / (1.0 + t)
    w2 = g_w * t / (1.0 + t)
    yield 8

    hit1 = lane == i1
    hit2 = lane == i2
    oh = jnp.where(jnp.logical_or(hit1, hit2), 1.0, 0.0)
    rr = lax.broadcasted_iota(I32, (tm, tm), 0)
    cc = lax.broadcasted_iota(I32, (tm, tm), 1)
    ltri = jnp.where(cc < rr, 1.0, 0.0).astype(BF16)
    before = jnp.dot(ltri, oh.astype(BF16), preferred_element_type=F32) + cnt_ref[0:1, :]
    rank1 = jnp.sum(jnp.where(hit1, before, 0.0), axis=-1, keepdims=True)
    rank2 = jnp.sum(jnp.where(hit2, before, 0.0), axis=-1, keepdims=True)
    cnt_ref[...] = cnt_ref[...] + jnp.sum(oh, axis=0, keepdims=True)

    e1 = (i1 - RLANE_E).astype(F32)
    e2 = (i2 - RLANE_E).astype(F32)
    cols = (e1, e2, w1, w2, rank1, rank2)
    info = jnp.zeros((tm, LANES), F32)
    for c, val in enumerate(cols):
        info = jnp.where(lane == c, val, info)
    ri_ref[rows, :] = info


def _post(x, sb, cv, w_out, gx, w_xq, kt, vx, w_xo, gf, w_router, b_router):
    B, S, _ = x.shape
    T = B * S
    tm = TM_POST
    spt = S // tm
    tok = lambda width: pl.BlockSpec((1, tm, width), lambda b, i: (b, i, 0))
    flat = lambda width: pl.BlockSpec((tm, width), lambda b, i: (b * spt + i, 0))
    const = lambda r, c: pl.BlockSpec((r, c), lambda b, i: (0, 0))
    M = vx.shape[1]
    return pl.pallas_call(
        _post_kernel,
        grid=(B, spt),
        in_specs=[tok(D_MODEL), tok(D_SB), tok(D_CONV),
                  const(D_MODEL, D_MODEL), const(1, D_MODEL), const(D_MODEL, D_MODEL),
                  pl.BlockSpec((1, D_MODEL, M), lambda b, i: (b, 0, 0)),
                  pl.BlockSpec((1, M, D_MODEL), lambda b, i: (b, 0, 0)),
                  const(D_MODEL, D_MODEL), const(1, D_MODEL),
                  const(D_MODEL, LANES), const(1, LANES)],
        out_specs=[tok(D_MODEL), pl.BlockSpec((tm * ROW_CHUNKS, LANES), lambda b, i: (b * spt + i, 0)), flat(LANES),
                   pl.BlockSpec((SUBLANES, LANES), lambda b, i: (0, 0))],
        out_shape=[jax.ShapeDtypeStruct((B, S, D_MODEL), F32),
                   jax.ShapeDtypeStruct((T * ROW_CHUNKS, LANES), U32),
                   jax.ShapeDtypeStruct((T, LANES), F32),
                   jax.ShapeDtypeStruct((SUBLANES, LANES), F32)],
        compiler_params=_cparams(2),
        name="post",
    )(x, sb, cv, w_out, gx.reshape(1, D_MODEL), w_xq, kt, vx, w_xo, gf.reshape(1, D_MODEL),
      w_router, b_router)


def _token_rows(ref, token):
    return ref.at[pl.ds(pl.multiple_of(token * ROW_CHUNKS, ROW_CHUNKS), ROW_CHUNKS)]


def _row_copy_wait(src_rows, dst_rows, sem, n):
    pltpu.make_async_copy(src_rows.at[pl.ds(0, n * ROW_CHUNKS)], dst_rows.at[pl.ds(0, n * ROW_CHUNKS)], sem).wait()


def _dispatch_kernel(pend_ref, dest_ref, hp_ref, xd_ref, idx_ref, zero_ref, sem_idx, sem_rows, sem_zero):
    tm = TM_ROWS
    i = pl.program_id(0)
    n = pl.num_programs(0)
    slot = i % 2

    def idx_copy(step, s):
        return pltpu.make_async_copy(dest_ref.at[step], idx_ref.at[s], sem_idx.at[s])

    def zero_copy(e):
        start = pl.multiple_of((pend_ref[e] - TM_EXP) * ROW_CHUNKS, TM_EXP * ROW_CHUNKS)
        return pltpu.make_async_copy(zero_ref, xd_ref.at[pl.ds(start, TM_EXP * ROW_CHUNKS)], sem_zero)

    def nonempty(e):
        return pend_ref[e] > (pend_ref[e - 1] if e else 0)

    @pl.when(i == 0)
    def _():
        idx_copy(0, 0).start()
        zero_ref[...] = jnp.zeros_like(zero_ref)
        for e in range(N_EXPERTS):
            pl.when(nonempty(e))(lambda e=e: zero_copy(e).start())
        for e in range(N_EXPERTS):
            pl.when(nonempty(e))(lambda e=e: zero_copy(e).wait())

        def zero_unused_tile(j, _):
            start = pl.multiple_of(j * (TM_EXP * ROW_CHUNKS), TM_EXP * ROW_CHUNKS)
            cp = pltpu.make_async_copy(zero_ref, xd_ref.at[pl.ds(start, TM_EXP * ROW_CHUNKS)], sem_zero)
            cp.start()
            cp.wait()
            return 0

        n_tiles = xd_ref.shape[0] // (TM_EXP * ROW_CHUNKS)
        lax.fori_loop(pend_ref[N_EXPERTS - 1] // TM_EXP, n_tiles, zero_unused_tile, 0)

    idx_copy(i, slot).wait()

    @pl.when(i + 1 < n)
    def _():
        idx_copy(i + 1, 1 - slot).start()

    for h in range(DISPATCH_TILES):
        for r0 in range(0, tm, ISSUE_BATCH):
            batch = [(r, k, idx_ref[slot, (2 * h + k) * tm + r]) for r in range(r0, r0 + ISSUE_BATCH) for k in range(2)]
            for r, k, d in batch:
                pltpu.make_async_copy(_token_rows(hp_ref, h * tm + r), _token_rows(xd_ref, d),
                                      sem_rows).start(priority=k)
    for _ in range(2 * DISPATCH_TILES):
        _row_copy_wait(hp_ref, xd_ref, sem_rows, tm)


def _dispatch(pend, dest, hp, n_rows):
    T = hp.shape[0] // ROW_CHUNKS
    tm = TM_ROWS
    grid_spec = pltpu.PrefetchScalarGridSpec(
        num_scalar_prefetch=1,
        grid=(T // (DISPATCH_TILES * tm),),
        in_specs=[pl.BlockSpec(memory_space=pl.ANY),
                  pl.BlockSpec((DISPATCH_TILES * tm * ROW_CHUNKS, LANES), lambda i, pend: (i, 0))],
        out_specs=pl.BlockSpec(memory_space=pl.ANY),
        scratch_shapes=[pltpu.SMEM((2, 2 * DISPATCH_TILES * tm), I32),
                        pltpu.VMEM((TM_EXP * ROW_CHUNKS, LANES), U32),
                        pltpu.SemaphoreType.DMA((2,)), pltpu.SemaphoreType.DMA, pltpu.SemaphoreType.DMA],
    )
    return pl.pallas_call(
        _dispatch_kernel,
        grid_spec=grid_spec,
        out_shape=jax.ShapeDtypeStruct((n_rows * ROW_CHUNKS, LANES), U32),
        compiler_params=_cparams(1),
        name="dispatch",
    )(pend, dest.reshape(-1, 2 * DISPATCH_TILES * tm), hp)


def _experts_kernel(te_ref, nv_ref, xd_ref, wg_ref, wu_ref, wdn_ref, yd_ref, wgu_ref, wd_ref):
    i = pl.program_id(0)
    sub_rows = TM_EXP // EXP_SPLIT
    used = i < nv_ref[0]
    new_expert = jnp.logical_or(i == 0, te_ref[i] != te_ref[jnp.maximum(i - 1, 0)])

    @pl.when(jnp.logical_and(used, new_expert))
    def _():
        wgu_ref[:, :D_EXPERT] = wg_ref[0].astype(BF16)
        wgu_ref[:, D_EXPERT:] = wu_ref[0].astype(BF16)
        wd_ref[...] = wdn_ref[0].astype(BF16)

    def rows_pass(part):
        lo, hi = _unpack_rows(_load_token_rows(xd_ref, part * sub_rows, sub_rows))
        lo, hi = lo.astype(BF16), hi.astype(BF16)
        yield 1
        gu = (jnp.dot(lo, wgu_ref[:HALF, :], preferred_element_type=F32)
              + jnp.dot(hi, wgu_ref[HALF:, :], preferred_element_type=F32))
        yield 2
        gate = gu[:, :D_EXPERT]
        up = gu[:, D_EXPERT:]
        hmid = (gate * jax.nn.sigmoid(gate) * up).astype(BF16)
        yield 3
        y = jnp.dot(hmid, wd_ref[...], preferred_element_type=F32)
        yield 4
        _store_token_rows(yd_ref, part * sub_rows, _pack_rows(y))

    @pl.when(jnp.logical_not(used))
    def _():
        yd_ref[...] = jnp.zeros_like(yd_ref)

    @pl.when(used)
    def _():
        groups = [rows_pass(part) for part in range(EXP_SPLIT)]
        live = True
        step = 0
        while live:
            live = False
            for part, group in enumerate(groups):
                if step >= part and next(group, None) is not None:
                    live = True
            step += 1


def _experts(tile_expert, n_valid, xd, w_gate, w_up, w_down):
    n_tiles = tile_expert.shape[0]
    tm = TM_EXP
    tile = lambda i, te, nv: (jnp.minimum(i, nv[0] - 1), 0)
    weight = lambda i, te, nv: (te[jnp.minimum(i, nv[0] - 1)], 0, 0)
    grid_spec = pltpu.PrefetchScalarGridSpec(
        num_scalar_prefetch=2,
        grid=(n_tiles,),
        in_specs=[pl.BlockSpec((tm * ROW_CHUNKS, LANES), tile),
                  pl.BlockSpec((1, D_MODEL, D_EXPERT), weight),
                  pl.BlockSpec((1, D_MODEL, D_EXPERT), weight),
                  pl.BlockSpec((1, D_EXPERT, D_MODEL), weight)],
        out_specs=pl.BlockSpec((tm * ROW_CHUNKS, LANES), lambda i, te, nv: (i, 0)),
        scratch_shapes=[pltpu.VMEM((D_MODEL, 2 * D_EXPERT), BF16), pltpu.VMEM((D_EXPERT, D_MODEL), BF16)],
    )
    return pl.pallas_call(
        _experts_kernel,
        grid_spec=grid_spec,
        out_shape=jax.ShapeDtypeStruct((n_tiles * tm * ROW_CHUNKS, LANES), U32),
        compiler_params=_cparams(1),
        name="experts",
    )(tile_expert, n_valid, xd, w_gate, w_up, w_down)


def _combine_kernel(dest_ref, yd_ref, x2_ref, ri_ref, g_ref, o_ref, idx_ref, rows_ref, sem_idx, sem_rows):
    tm = TM_ROWS
    j = pl.program_id(0)
    n = pl.num_programs(0)
    last = 2 * n - 1

    def idx_copy(tile, s):
        return pltpu.make_async_copy(dest_ref.at[jnp.minimum(tile, last)], idx_ref.at[s], sem_idx.at[s])

    def issue_rows(s, first, count):
        for r0 in range(first, first + count, ISSUE_BATCH):
            batch = [(r, k, idx_ref[s, k * tm + r]) for r in range(r0, r0 + ISSUE_BATCH) for k in range(2)]
            for r, k, d in batch:
                pltpu.make_async_copy(_token_rows(yd_ref, d), _token_rows(rows_ref.at[s, k], r),
                                      sem_rows.at[s]).start(priority=k)

    def wait_rows(s):
        for k in range(2):
            _row_copy_wait(yd_ref, rows_ref.at[s, k], sem_rows.at[s], tm)

    def consume(s, tile):
        idx_copy(tile + 1, 1 - s).wait()
        wait_rows(s)
        idx_copy(tile + 2, s).start()
        crows = tm // COMBINE_CHUNKS
        issuing = COMBINE_CHUNKS // 2
        per_chunk = tm // issuing
        for c in range(COMBINE_CHUNKS):
            rows = slice(s * tm + c * crows, s * tm + (c + 1) * crows)
            ri = ri_ref[rows, :]
            y_lo = jnp.zeros((crows, HALF), F32)
            y_hi = jnp.zeros((crows, HALF), F32)
            for k in range(2):
                lo, hi = _unpack_rows(_load_token_rows(rows_ref.at[s, k], c * crows, crows))
                wk = ri[:, 2 + k:3 + k]
                y_lo = y_lo + wk * lo
                y_hi = y_hi + wk * hi
            x3 = x2_ref[rows, :] + jnp.concatenate([y_lo, y_hi], axis=1)
            o_ref[rows, :] = _rms(x3, g_ref[...])
            if c < issuing:
                issue_rows(1 - s, c * per_chunk, per_chunk)

    @pl.when(j == 0)
    def _():
        first_idx = idx_copy(0, 0)
        first_idx.start()
        first_idx.wait()
        issue_rows(0, 0, tm)
        idx_copy(1, 1).start()

    consume(0, 2 * j)
    consume(1, 2 * j + 1)

    @pl.when(j == n - 1)
    def _():
        idx_copy(last, 1).wait()
        wait_rows(0)


def _combine(dest, yd, x2, rinfo, g):
    T = x2.shape[0]
    tm = TM_ROWS
    return pl.pallas_call(
        _combine_kernel,
        grid=(T // (2 * tm),),
        in_specs=[pl.BlockSpec(memory_space=pl.ANY),
                  pl.BlockSpec(memory_space=pl.ANY),
                  pl.BlockSpec((2 * tm, D_MODEL), lambda i: (i, 0)),
                  pl.BlockSpec((2 * tm, LANES), lambda i: (i, 0)),
                  pl.BlockSpec((1, D_MODEL), lambda i: (0, 0))],
        out_specs=pl.BlockSpec((2 * tm, D_MODEL), lambda i: (i, 0)),
        out_shape=jax.ShapeDtypeStruct((T, D_MODEL), F32),
        scratch_shapes=[pltpu.SMEM((2, 2 * tm), I32), pltpu.VMEM((2, 2, tm * ROW_CHUNKS, LANES), U32),
                        pltpu.SemaphoreType.DMA((2,)), pltpu.SemaphoreType.DMA((2,))],
        compiler_params=_cparams(1),
        name="combine",
    )(dest, yd, x2, rinfo, g.reshape(1, D_MODEL))


def _routing_tables(rinfo, counts_row, T):
    tm = TM_EXP
    counts = counts_row[RLANE_E:RLANE_E + N_EXPERTS].astype(I32)
    padded = (counts + tm - 1) // tm * tm
    pend = jnp.cumsum(padded)
    pstart = pend - padded
    e = rinfo[:, 0:2].astype(I32)
    rank = rinfo[:, 4:6].astype(I32)
    ids = jnp.arange(N_EXPERTS, dtype=I32)
    dest = rank + jnp.sum(jnp.where(e[..., None] == ids, pstart, 0), axis=-1)
    n_tiles = (2 * T) // tm + N_EXPERTS
    tile_start = jnp.arange(n_tiles, dtype=I32) * tm
    tile_expert = jnp.minimum(jnp.sum((tile_start[:, None] >= pend[None, :]).astype(I32), axis=1),
                              N_EXPERTS - 1)
    n_valid = (pend[-1:] // tm).astype(I32)
    ntt = T // TM_ROWS
    dest_tiles = dest.reshape(ntt, TM_ROWS, 2).transpose(0, 2, 1).reshape(ntt, 2 * TM_ROWS)
    return dest_tiles, tile_expert, n_valid, pend.astype(I32), n_tiles


def kernel(x, mem, ln_mix_g, w_in, sb_out_g, conv_w, conv_b, conv_ln_g, conv_ln_b, w_out, ln_mem_x_g,
           ln_mem_g, w_xq, w_xkv, w_xo, ln_ffn_g, w_group, b_group, w_er, b_er, w_gate, w_up, w_down,
           ln_final_g):
    B, S, D = x.shape
    T = B * S
    assert ln_mix_g.shape[0] == 1, "single-layer pipeline: the output norm is fused into the combine kernel"
    assert D == D_MODEL and S % TS_CONV == 0 and S % TM_PROJ == 0 and T % (2 * TM_ROWS) == 0
    l = 0
    qt4, k5, vt5, u = _inproj(x, ln_mix_g[l], w_in[l].astype(BF16))
    sb = _sb_attention(qt4, k5, vt5, sb_out_g[l])
    cv = _conv_branch(u, conv_w[l], conv_b[l], conv_ln_g[l], conv_ln_b[l])
    kt, vx = _memkv(mem, ln_mem_g[l], w_xkv[l].astype(BF16))

    pad = LANES - RLANE_E - N_EXPERTS
    w_router = jnp.concatenate([w_group[l], w_er[l].reshape(D, N_EXPERTS), jnp.zeros((D, pad), F32)], axis=1)
    b_router = jnp.concatenate([b_group[l], b_er[l].reshape(N_EXPERTS), jnp.zeros((pad,), F32)])
    x2, hp, rinfo, cnt = _post(x, sb, cv, w_out[l].astype(BF16), ln_mem_x_g[l], w_xq[l].astype(BF16),
                               kt, vx, w_xo[l].astype(BF16), ln_ffn_g[l],
                               w_router.astype(BF16), b_router.reshape(1, LANES))

    dest_tiles, tile_expert, n_valid, pend, n_tiles = _routing_tables(rinfo, cnt[0], T)
    xd = _dispatch(pend, dest_tiles, hp, n_tiles * TM_EXP)
    yd = _experts(tile_expert, n_valid, xd, w_gate[l], w_up[l], w_down[l])
    return _combine(dest_tiles, yd, x2.reshape(T, D), rinfo, ln_final_g).reshape(B, S, D)
```

```python
import functools

import jax
import numpy as np
import jax.numpy as jnp
from jax import lax
from jax.experimental import pallas as pl
from jax.experimental.pallas import tpu as pltpu

F32 = jnp.float32
BF16 = jnp.bfloat16
I32 = jnp.int32
U32 = jnp.uint32

EPS = 1e-6

D_MODEL = 1024
N_SB_HEADS = 8
SB_HEAD_DIM = 64
D_SB = N_SB_HEADS * SB_HEAD_DIM
D_CONV = D_MODEL - D_SB
CONV_WIDTH = 31
N_XHEADS = 4
XHEAD_DIM = D_MODEL // N_XHEADS
N_GROUPS = 4
EXPERTS_PER_GROUP = 8
N_EXPERTS = N_GROUPS * EXPERTS_PER_GROUP
D_EXPERT = 512

LANES = 128
SUBLANES = 8
VMEM_LIMIT = 56 * 1024 * 1024

TM_PROJ = 512
SB_BLK = 256
SB_GROUP = SB_BLK // SUBLANES
SB_BODY_ITEMS = 6
TS_CONV = 512
CONV_HALO = 32
CONV_CHUNK = 32
CONV_BODY_CHUNKS = 2
CONV_L = TS_CONV // SUBLANES
CONV_EXT_ROWS = (CONV_L + CONV_HALO) * SUBLANES
CONV_PITCH_IN = 100
CONV_PITCH_OUT = 68
TM_POST = 512
POST_SPLIT = 2
TM_ROWS = 512
TM_EXP = 512
EXP_SPLIT = 2
ISSUE_BATCH = 8
DISPATCH_TILES = 2
COMBINE_CHUNKS = 8
HALF = D_MODEL // 2
ROW_CHUNKS = HALF // LANES
RLANE_E = 4
NEG_BIG = -1e30
LOG2E = 1.4426950408889634
LOGIT_CAP = 64.0


def _cparams(n_axes):
    return pltpu.CompilerParams(dimension_semantics=("arbitrary",) * n_axes,
                                vmem_limit_bytes=VMEM_LIMIT)


def _rms(x, g):
    ms = jnp.mean(x * x, axis=-1, keepdims=True)
    return x * lax.rsqrt(ms + EPS) * g


def _pack_rows(y):
    lo = pltpu.bitcast(y[:, :HALF].astype(BF16).astype(F32), U32)
    hi = pltpu.bitcast(y[:, HALF:].astype(BF16).astype(F32), U32)
    return (lo >> 16) | (hi & jnp.uint32(0xFFFF0000))


def _unpack_rows(w):
    lo = pltpu.bitcast(w << 16, F32)
    hi = pltpu.bitcast(w & jnp.uint32(0xFFFF0000), F32)
    return lo, hi


def _store_token_rows(ref, first, words):
    rows = words.shape[0]
    for c in range(ROW_CHUNKS):
        ref[pl.ds(first * ROW_CHUNKS + c, rows, stride=ROW_CHUNKS), :] = words[:, c * LANES:(c + 1) * LANES]


def _load_token_rows(ref, first, rows):
    return jnp.concatenate([ref[pl.ds(first * ROW_CHUNKS + c, rows, stride=ROW_CHUNKS), :]
                            for c in range(ROW_CHUNKS)], axis=1)


def _inproj_kernel(x_ref, g_ref, w_ref, qt_ref, k_ref, vt_ref, u_ref, perm_ref):
    blk = SB_BLK
    npair = N_SB_HEADS // 2

    @pl.when(jnp.logical_and(pl.program_id(0) == 0, pl.program_id(1) == 0))
    def _():
        p = lax.broadcasted_iota(I32, (blk, blk), 0)
        c = lax.broadcasted_iota(I32, (blk, blk), 1)
        perm_ref[...] = jnp.where(c == (p & (SUBLANES - 1)) * SB_GROUP + (p >> 3), 1.0, 0.0).astype(BF16)

    h = _rms(x_ref[0], g_ref[...]).astype(BF16)
    p = jnp.dot(h, w_ref[...], preferred_element_type=F32)
    u_ref[0] = p[:, 3 * D_SB:].astype(BF16)
    q = p[:, :D_SB] * (SB_HEAD_DIM ** -0.5 * LOG2E)
    for j in range(npair):
        qt_ref[0, j] = q[:, j * LANES:(j + 1) * LANES].T.astype(BF16)
    for nb in range(TM_PROJ // blk):
        rows = slice(nb * blk, (nb + 1) * blk)
        kv = p[rows, D_SB:3 * D_SB].astype(BF16)
        kvp = jnp.dot(perm_ref[...], kv, preferred_element_type=F32)
        for j in range(npair):
            k_ref[0, j, nb] = kvp[:, j * LANES:(j + 1) * LANES].astype(BF16)
            vt = kvp[:, D_SB + j * LANES:D_SB + (j + 1) * LANES].T.astype(BF16)
            vt_ref[0, 2 * j, nb] = vt[:SB_HEAD_DIM]
            vt_ref[0, 2 * j + 1, nb] = vt[SB_HEAD_DIM:]


def _inproj(x, g, w):
    B, S, _ = x.shape
    n_out = w.shape[1]
    tm = TM_PROJ
    blk = SB_BLK
    nb = tm // blk
    npair = N_SB_HEADS // 2
    return pl.pallas_call(
        _inproj_kernel,
        grid=(B, S // tm),
        in_specs=[pl.BlockSpec((1, tm, D_MODEL), lambda b, i: (b, i, 0)),
                  pl.BlockSpec((1, D_MODEL), lambda b, i: (0, 0)),
                  pl.BlockSpec((D_MODEL, n_out), lambda b, i: (0, 0))],
        out_specs=[pl.BlockSpec((1, npair, LANES, tm), lambda b, i: (b, 0, 0, i)),
                   pl.BlockSpec((1, npair, nb, blk, LANES), lambda b, i: (b, 0, i, 0, 0)),
                   pl.BlockSpec((1, N_SB_HEADS, nb, SB_HEAD_DIM, blk), lambda b, i: (b, 0, i, 0, 0)),
                   pl.BlockSpec((1, tm, 2 * D_CONV), lambda b, i: (b, i, 0))],
        out_shape=[jax.ShapeDtypeStruct((B, npair, LANES, S), BF16),
                   jax.ShapeDtypeStruct((B, npair, S // blk, blk, LANES), BF16),
                   jax.ShapeDtypeStruct((B, N_SB_HEADS, S // blk, SB_HEAD_DIM, blk), BF16),
                   jax.ShapeDtypeStruct((B, S, 2 * D_CONV), BF16)],
        scratch_shapes=[pltpu.VMEM((blk, blk), BF16)],
        compiler_params=_cparams(2),
        name="inproj",
    )(x, g.reshape(1, D_MODEL), w)


def _sb_item_table(nqb):
    items = [(qb, kb, head, 1 if kb == qb else 0)
             for qb in range(nqb) for kb in range(qb, -1, -1) for head in range(N_SB_HEADS)]
    n_items = len(items)
    trips = -(-(n_items + 2) // SB_BODY_ITEMS)
    items += [items[-1][:3] + (2,)] * (SB_BODY_ITEMS * trips + 2 - n_items)
    return np.asarray(items, np.int32).T.copy(), n_items, trips


def _sb_kernel(tab_ref, qt_ref, k_ref, vt_ref, g_ref, o_ref, ot_ref, carry_ref, qmt_ref, bias_ref,
               z0, z1, z2, e0, e1, e2, w0, w1, w2, at0, at1, at2, *, nqb, trips):
    blk = SB_BLK

    @pl.when(pl.program_id(0) == 0)
    def _():
        rowp = lax.broadcasted_iota(I32, (blk, blk), 0)
        colc = lax.broadcasted_iota(I32, (blk, blk), 1)
        key_local = (rowp & (SUBLANES - 1)) * SB_GROUP + (rowp >> 3)
        bias_ref[0] = jnp.zeros((blk, blk), F32)
        bias_ref[1] = jnp.where(key_local < colc, 0.0, NEG_BIG)
        bias_ref[2] = jnp.full((blk, blk), NEG_BIG, F32)

    ot_ref[...] = jnp.zeros_like(ot_ref)
    carry_ref[...] = jnp.ones_like(carry_ref)
    at1[...] = jnp.zeros_like(at1)
    at2[...] = jnp.zeros_like(at2)
    e2[...] = jnp.zeros_like(e2)
    w2[...] = jnp.zeros_like(w2)
    rowi = lax.broadcasted_iota(I32, (LANES, blk), 0)
    for qb in range(nqb):
        for j in range(N_SB_HEADS // 2):
            qs = qt_ref[0, j, :, qb * blk:(qb + 1) * blk]
            zero = jnp.zeros_like(qs)
            qmt_ref[qb * N_SB_HEADS + 2 * j] = jnp.where(rowi < SB_HEAD_DIM, qs, zero)
            qmt_ref[qb * N_SB_HEADS + 2 * j + 1] = jnp.where(rowi >= SB_HEAD_DIM, qs, zero)
    sub = lax.broadcasted_iota(I32, (SUBLANES, blk), 0)

    def item(m):
        return tab_ref[0, m], tab_ref[1, m], tab_ref[2, m]

    def z_phase(m, zw):
        qb, kb, head = item(m)
        zw[...] = jnp.dot(k_ref[0, head >> 1, kb], qmt_ref[qb * N_SB_HEADS + head],
                          preferred_element_type=F32)

    def e1_phase(m, zr, er, wr):
        qb, kb, head = item(m)
        bidx = tab_ref[3, m]
        state = qb * N_SB_HEADS + head
        pex = jnp.ones((SUBLANES, blk), F32)
        for i in reversed(range(SB_GROUP)):
            rows = slice(SUBLANES * i, SUBLANES * (i + 1))
            z = zr[rows, :] + bias_ref[bidx, rows, :]
            t = jnp.exp2(jnp.minimum(z, LOGIT_CAP))
            keep = 1.0 / (1.0 + t)
            er[rows, :] = (t * keep) * pex
            pex = pex * keep
        incl = pex
        for s in (1, 2, 4):
            shifted = pltpu.roll(incl, SUBLANES - s, 0)
            incl = incl * jnp.where(sub + s < SUBLANES, shifted, 1.0)
        later = jnp.where(sub + 1 < SUBLANES, pltpu.roll(incl, SUBLANES - 1, 0), 1.0)
        carry = carry_ref[state]
        w = carry * later
        carry_ref[state] = carry * jnp.broadcast_to(incl[0:1], (SUBLANES, blk))
        wr[...] = jnp.concatenate([w, w], axis=0)

    def e2_phase(er, wr, atw):
        scale = wr[...]
        for i in range(0, SB_GROUP, 2):
            rows = slice(SUBLANES * i, SUBLANES * (i + 2))
            atw[rows, :] = (er[rows, :] * scale).astype(BF16)

    def av_phase(m, atr):
        qb, kb, head = item(jnp.maximum(m - 2, 0))
        state = qb * N_SB_HEADS + head
        ot_ref[state] = ot_ref[state] + jnp.dot(vt_ref[0, head, kb], atr[...], preferred_element_type=F32)

    zs, es, ws, ats = (z0, z1, z2), (e0, e1, e2), (w0, w1, w2), (at0, at1, at2)
    z_phase(0, z0)
    z_phase(1, z1)

    def body(t, _):
        for u in range(SB_BODY_ITEMS):
            m = SB_BODY_ITEMS * t + u
            z_phase(m + 2, zs[(u + 2) % 3])
            e1_phase(m, zs[u % 3], es[u % 3], ws[u % 3])
            e2_phase(es[(u + 2) % 3], ws[(u + 2) % 3], ats[(u + 2) % 3])
            av_phase(m, ats[(u + 1) % 3])
        return 0

    lax.fori_loop(0, trips, body, 0)
    for qb in range(nqb):
        first = qb * N_SB_HEADS
        parts = [ot_ref[first + 2 * j:first + 2 * j + 2].reshape(LANES, blk).T for j in range(N_SB_HEADS // 2)]
        o_ref[0, qb * blk:(qb + 1) * blk, :] = _rms(jnp.concatenate(parts, axis=1), g_ref[...]).astype(BF16)


def _sb_attention(qt4, k5, vt5, g):
    B, npair, _, S = qt4.shape
    nkb = S // SB_BLK
    blk = SB_BLK
    table, n_items, trips = _sb_item_table(nkb)
    del n_items
    grid_spec = pltpu.PrefetchScalarGridSpec(
        num_scalar_prefetch=1,
        grid=(B,),
        in_specs=[pl.BlockSpec((1, npair, LANES, S), lambda b, tab: (b, 0, 0, 0)),
                  pl.BlockSpec((1, npair, nkb, blk, LANES), lambda b, tab: (b, 0, 0, 0, 0)),
                  pl.BlockSpec((1, N_SB_HEADS, nkb, SB_HEAD_DIM, blk), lambda b, tab: (b, 0, 0, 0, 0)),
                  pl.BlockSpec((1, D_SB), lambda b, tab: (0, 0))],
        out_specs=pl.BlockSpec((1, S, D_SB), lambda b, tab: (b, 0, 0)),
        scratch_shapes=[pltpu.VMEM((nkb * N_SB_HEADS, SB_HEAD_DIM, blk), F32),
                        pltpu.VMEM((nkb * N_SB_HEADS, SUBLANES, blk), F32),
                        pltpu.VMEM((nkb * N_SB_HEADS, LANES, blk), BF16),
                        pltpu.VMEM((3, blk, blk), F32),
                        ] + [pltpu.VMEM((blk, blk), F32)] * 6
                        + [pltpu.VMEM((2 * SUBLANES, blk), F32)] * 3
                        + [pltpu.VMEM((blk, blk), BF16)] * 3,
    )
    return pl.pallas_call(
        functools.partial(_sb_kernel, nqb=nkb, trips=trips),
        grid_spec=grid_spec,
        out_shape=jax.ShapeDtypeStruct((B, S, D_SB), BF16),
        compiler_params=_cparams(1),
        name="sb_attn",
    )(jnp.asarray(table), qt4, k5, vt5, g.reshape(1, D_SB))


def _conv_kernel(u_ref, uh_ref, cw_ref, cb_ref, lg_ref, lb_ref, o_ref, cwb_ref, gs_ref, gext_ref, yp_ref, os_ref):
    step = pl.program_id(1)
    ts = TS_CONV
    nslab = D_CONV // LANES
    seg = CONV_L + CONV_HALO

    @pl.when(jnp.logical_and(pl.program_id(0) == 0, step == 0))
    def _():
        for w in range(CONV_WIDTH):
            cwb_ref[w] = jnp.broadcast_to(cw_ref[w:w + 1, :], (SUBLANES, D_CONV))

    def glu(u):
        a = u[:, :D_CONV].astype(F32)
        gate = u[:, D_CONV:].astype(F32)
        return a * jax.nn.sigmoid(gate)

    gh = jnp.where(step > 0, glu(uh_ref[0]), 0.0)
    g = jnp.concatenate([gh, glu(u_ref[0])], axis=0)
    for r in range(SUBLANES):
        for l in range(nslab):
            gs_ref[l, r * CONV_PITCH_IN:r * CONV_PITCH_IN + seg, :] = (
                g[r * CONV_L:r * CONV_L + seg, l * LANES:(l + 1) * LANES])
    for e in range(seg):
        for l in range(nslab):
            gext_ref[e * SUBLANES:(e + 1) * SUBLANES, l * LANES:(l + 1) * LANES] = (
                gs_ref[l, pl.ds(e, SUBLANES, stride=CONV_PITCH_IN), :])
    first = CONV_HALO - (CONV_WIDTH - 1)
    rows = CONV_CHUNK

    def chunk(c, _):
        for half in range(CONV_BODY_CHUNKS):
            r0 = pl.multiple_of((c * CONV_BODY_CHUNKS + half) * rows, rows)
            acc = jnp.zeros((rows // SUBLANES, SUBLANES, D_CONV), F32)
            for w in range(CONV_WIDTH):
                start = pl.multiple_of(r0 + (first + w) * SUBLANES, SUBLANES)
                gw = gext_ref[pl.ds(start, rows), :].reshape(rows // SUBLANES, SUBLANES, D_CONV)
                acc = acc + gw * cwb_ref[w]
            yp_ref[pl.ds(r0, rows), :] = acc.reshape(rows, D_CONV)
        return 0

    lax.fori_loop(0, ts // (rows * CONV_BODY_CHUNKS), chunk, 0)
    y = yp_ref[...] + cb_ref[...]
    mu = jnp.mean(y, axis=-1, keepdims=True)
    d = y - mu
    var = jnp.mean(d * d, axis=-1, keepdims=True)
    yn = d * lax.rsqrt(var + EPS) * lg_ref[...] + lb_ref[...]
    act = yn * jax.nn.sigmoid(yn)
    for i in range(CONV_L):
        for l in range(nslab):
            os_ref[l, pl.ds(i, SUBLANES, stride=CONV_PITCH_OUT), :] = (
                act[i * SUBLANES:(i + 1) * SUBLANES, l * LANES:(l + 1) * LANES])
    for r in range(SUBLANES):
        for l in range(nslab):
            o_ref[0, r * CONV_L:(r + 1) * CONV_L, l * LANES:(l + 1) * LANES] = (
                os_ref[l, r * CONV_PITCH_OUT:r * CONV_PITCH_OUT + CONV_L, :].astype(BF16))


def _conv_branch(u3, conv_w, conv_b, ln_g, ln_b):
    B, S, _ = u3.shape
    ts = TS_CONV
    halo_per_step = ts // CONV_HALO
    vec = lambda a: a.reshape(1, D_CONV)
    cw = jnp.concatenate([conv_w, jnp.zeros((CONV_HALO - CONV_WIDTH, D_CONV), F32)], axis=0)
    const = lambda rows: pl.BlockSpec((rows, D_CONV), lambda b, i: (0, 0))
    return pl.pallas_call(
        _conv_kernel,
        grid=(B, S // ts),
        in_specs=[pl.BlockSpec((1, ts, 2 * D_CONV), lambda b, i: (b, i, 0)),
                  pl.BlockSpec((1, CONV_HALO, 2 * D_CONV),
                               lambda b, i: (b, jnp.maximum(i * halo_per_step - 1, 0), 0)),
                  const(CONV_HALO), const(1), const(1), const(1)],
        out_specs=pl.BlockSpec((1, ts, D_CONV), lambda b, i: (b, i, 0)),
        out_shape=jax.ShapeDtypeStruct((B, S, D_CONV), BF16),
        scratch_shapes=[pltpu.VMEM((CONV_HALO, SUBLANES, D_CONV), F32),
                        pltpu.VMEM((D_CONV // LANES, SUBLANES * CONV_PITCH_IN, LANES), F32),
                        pltpu.VMEM((CONV_EXT_ROWS, D_CONV), F32),
                        pltpu.VMEM((ts, D_CONV), F32),
                        pltpu.VMEM((D_CONV // LANES, SUBLANES * CONV_PITCH_OUT, LANES), F32)],
        compiler_params=_cparams(2),
        name="conv",
    )(u3, u3, cw, vec(conv_b), vec(ln_g), vec(ln_b))


def _memkv_kernel(m_ref, g_ref, w_ref, kt_ref, v_ref):
    mn = _rms(m_ref[0], g_ref[...]).astype(BF16)
    kv = jnp.dot(mn, w_ref[...], preferred_element_type=F32)
    kt_ref[0] = kv[:, :D_MODEL].T.astype(BF16)
    v_ref[0] = kv[:, D_MODEL:].astype(BF16)


def _memkv(mem, g, w):
    B, M, _ = mem.shape
    return pl.pallas_call(
        _memkv_kernel,
        grid=(B,),
        in_specs=[pl.BlockSpec((1, M, D_MODEL), lambda b: (b, 0, 0)),
                  pl.BlockSpec((1, D_MODEL), lambda b: (0, 0)),
                  pl.BlockSpec((D_MODEL, 2 * D_MODEL), lambda b: (0, 0))],
        out_specs=[pl.BlockSpec((1, D_MODEL, M), lambda b: (b, 0, 0)),
                   pl.BlockSpec((1, M, D_MODEL), lambda b: (b, 0, 0))],
        out_shape=[jax.ShapeDtypeStruct((B, D_MODEL, M), BF16),
                   jax.ShapeDtypeStruct((B, M, D_MODEL), BF16)],
        compiler_params=_cparams(1),
        name="memkv",
    )(mem, g.reshape(1, D_MODEL), w)


def _post_kernel(*refs):
    cnt_ref = refs[-1]

    @pl.when(jnp.logical_and(pl.program_id(0) == 0, pl.program_id(1) == 0))
    def _():
        cnt_ref[...] = jnp.zeros_like(cnt_ref)

    groups = [_post_rows(part, *refs) for part in range(POST_SPLIT)]
    live = True
    step = 0
    while live:
        live = False
        for part, group in enumerate(groups):
            if step >= part and next(group, None) is not None:
                live = True
        step += 1


def _post_rows(part, x_ref, sb_ref, cv_ref, wo_ref, gx_ref, wq_ref, kt_ref, vx_ref, wxo_ref, gf_ref,
               wr_ref, br_ref, x2_ref, hp_ref, ri_ref, it_ref, cnt_ref):
    tm = TM_POST // POST_SPLIT
    rows = slice(part * tm, (part + 1) * tm)

    x1 = (x_ref[0, rows, :]
          + jnp.dot(sb_ref[0, rows, :], wo_ref[:D_SB, :], preferred_element_type=F32)
          + jnp.dot(cv_ref[0, rows, :], wo_ref[D_SB:, :], preferred_element_type=F32))
    yield 1

    hq = _rms(x1, gx_ref[...]).astype(BF16)
    yield 2
    q = (jnp.dot(hq, wq_ref[...], preferred_element_type=F32) * (XHEAD_DIM ** -0.5)).astype(BF16)
    yield 3
    outs = []
    for h in range(N_XHEADS):
        hs = slice(h * XHEAD_DIM, (h + 1) * XHEAD_DIM)
        s = jnp.dot(q[:, hs], kt_ref[0, hs, :], preferred_element_type=F32)
        s = s - jnp.max(s, axis=-1, keepdims=True)
        p = jnp.exp(s)
        p = p / jnp.sum(p, axis=-1, keepdims=True)
        outs.append(jnp.dot(p.astype(BF16), vx_ref[0, :, hs], preferred_element_type=F32))
    o = jnp.concatenate(outs, axis=1).astype(BF16)
    yield 4
    x2 = x1 + jnp.dot(o, wxo_ref[...], preferred_element_type=F32)
    x2_ref[0, rows, :] = x2
    yield 5

    hn = _rms(x2, gf_ref[...])
    _store_token_rows(hp_ref, part * tm, _pack_rows(hn))
    yield 6
    logits = jnp.dot(hn.astype(BF16), wr_ref[...], preferred_element_type=F32) + br_ref[...]
    yield 7
    lane = lax.broadcasted_iota(I32, (tm, LANES), 1)
    big = jnp.int32(LANES)
    ninf = jnp.float32(-jnp.inf)

    def first_argmax(vals):
        m = jnp.max(vals, axis=-1, keepdims=True)
        idx = jnp.min(jnp.where(vals == m, lane, big), axis=-1, keepdims=True)
        return m, idx

    is_group = lane < N_GROUPS
    gl = jnp.where(is_group, logits, ninf)
    gmax, gidx = first_argmax(gl)
    gsum = jnp.sum(jnp.where(is_group, jnp.exp(logits - gmax), 0.0), axis=-1, keepdims=True)
    g_w = 1.0 / gsum
    lo = RLANE_E + gidx * EXPERTS_PER_GROUP
    in_group = jnp.logical_and(lane >= lo, lane < lo + EXPERTS_PER_GROUP)
    el = jnp.where(in_group, logits, ninf)
    v1, i1 = first_argmax(el)
    el2 = jnp.where(lane == i1, ninf, el)
    v2, i2 = first_argmax(el2)
    t = jnp.exp(v2 - v1)
    w1 = g_w / (1.0 + t)
    w2 = g_w * t / (1.0 + t)
    yield 8

    hit1 = lane == i1
    hit2 = lane == i2
    oh = jnp.where(jnp.logical_or(hit1, hit2), 1.0, 0.0)
    rr = lax.broadcasted_iota(I32, (tm, tm), 0)
    cc = lax.broadcasted_iota(I32, (tm, tm), 1)
    ltri = jnp.where(cc < rr, 1.0, 0.0).astype(BF16)
    before = jnp.dot(ltri, oh.astype(BF16), preferred_element_type=F32) + cnt_ref[0:1, :]
    rank1 = jnp.sum(jnp.where(hit1, before, 0.0), axis=-1, keepdims=True)
    rank2 = jnp.sum(jnp.where(hit2, before, 0.0), axis=-1, keepdims=True)
    cnt_ref[...] = cnt_ref[...] + jnp.sum(oh, axis=0, keepdims=True)

    e1 = (i1 - RLANE_E).astype(F32)
    e2 = (i2 - RLANE_E).astype(F32)
    cols = (e1, e2, w1, w2, rank1, rank2)
    info = jnp.zeros((tm, LANES), F32)
    for c, val in enumerate(cols):
        info = jnp.where(lane == c, val, info)
    ri_ref[rows, :] = info
    it_ref[:, rows] = info.T[:SUBLANES, :]


def _post(x, sb, cv, w_out, gx, w_xq, kt, vx, w_xo, gf, w_router, b_router):
    B, S, _ = x.shape
    T = B * S
    tm = TM_POST
    spt = S // tm
    tok = lambda width: pl.BlockSpec((1, tm, width), lambda b, i: (b, i, 0))
    flat = lambda width: pl.BlockSpec((tm, width), lambda b, i: (b * spt + i, 0))
    const = lambda r, c: pl.BlockSpec((r, c), lambda b, i: (0, 0))
    M = vx.shape[1]
    return pl.pallas_call(
        _post_kernel,
        grid=(B, spt),
        in_specs=[tok(D_MODEL), tok(D_SB), tok(D_CONV),
                  const(D_MODEL, D_MODEL), const(1, D_MODEL), const(D_MODEL, D_MODEL),
                  pl.BlockSpec((1, D_MODEL, M), lambda b, i: (b, 0, 0)),
                  pl.BlockSpec((1, M, D_MODEL), lambda b, i: (b, 0, 0)),
                  const(D_MODEL, D_MODEL), const(1, D_MODEL),
                  const(D_MODEL, LANES), const(1, LANES)],
        out_specs=[tok(D_MODEL), pl.BlockSpec((tm * ROW_CHUNKS, LANES), lambda b, i: (b * spt + i, 0)), flat(LANES),
                   pl.BlockSpec((SUBLANES, tm), lambda b, i: (0, b * spt + i)),
                   pl.BlockSpec((SUBLANES, LANES), lambda b, i: (0, 0))],
        out_shape=[jax.ShapeDtypeStruct((B, S, D_MODEL), F32),
                   jax.ShapeDtypeStruct((T * ROW_CHUNKS, LANES), U32),
                   jax.ShapeDtypeStruct((T, LANES), F32),
                   jax.ShapeDtypeStruct((SUBLANES, T), F32),
                   jax.ShapeDtypeStruct((SUBLANES, LANES), F32)],
        compiler_params=_cparams(2),
        name="post",
    )(x, sb, cv, w_out, gx.reshape(1, D_MODEL), w_xq, kt, vx, w_xo, gf.reshape(1, D_MODEL),
      w_router, b_router)


def _token_rows(ref, token):
    return ref.at[pl.ds(pl.multiple_of(token * ROW_CHUNKS, ROW_CHUNKS), ROW_CHUNKS)]


def _row_copy_wait(src_rows, dst_rows, sem, n):
    pltpu.make_async_copy(src_rows.at[pl.ds(0, n * ROW_CHUNKS)], dst_rows.at[pl.ds(0, n * ROW_CHUNKS)], sem).wait()


def _dispatch_kernel(pend_ref, dest_ref, hp_ref, xd_ref, idx_ref, zero_ref, sem_idx, sem_rows, sem_zero):
    tm = TM_ROWS
    i = pl.program_id(0)
    n = pl.num_programs(0)
    slot = i % 2

    def idx_copy(step, s):
        return pltpu.make_async_copy(dest_ref.at[step], idx_ref.at[s], sem_idx.at[s])

    def zero_copy(e):
        start = pl.multiple_of((pend_ref[e] - TM_EXP) * ROW_CHUNKS, TM_EXP * ROW_CHUNKS)
        return pltpu.make_async_copy(zero_ref, xd_ref.at[pl.ds(start, TM_EXP * ROW_CHUNKS)], sem_zero)

    def nonempty(e):
        return pend_ref[e] > (pend_ref[e - 1] if e else 0)

    @pl.when(i == 0)
    def _():
        idx_copy(0, 0).start()
        zero_ref[...] = jnp.zeros_like(zero_ref)
        for e in range(N_EXPERTS):
            pl.when(nonempty(e))(lambda e=e: zero_copy(e).start())
        for e in range(N_EXPERTS):
            pl.when(nonempty(e))(lambda e=e: zero_copy(e).wait())

        def zero_unused_tile(j, _):
            start = pl.multiple_of(j * (TM_EXP * ROW_CHUNKS), TM_EXP * ROW_CHUNKS)
            cp = pltpu.make_async_copy(zero_ref, xd_ref.at[pl.ds(start, TM_EXP * ROW_CHUNKS)], sem_zero)
            cp.start()
            cp.wait()
            return 0

        n_tiles = xd_ref.shape[0] // (TM_EXP * ROW_CHUNKS)
        lax.fori_loop(pend_ref[N_EXPERTS - 1] // TM_EXP, n_tiles, zero_unused_tile, 0)

    idx_copy(i, slot).wait()

    @pl.when(i + 1 < n)
    def _():
        idx_copy(i + 1, 1 - slot).start()

    for h in range(DISPATCH_TILES):
        for r0 in range(0, tm, ISSUE_BATCH):
            batch = [(r, k, idx_ref[slot, (2 * h + k) * tm + r]) for r in range(r0, r0 + ISSUE_BATCH) for k in range(2)]
            for r, k, d in batch:
                pltpu.make_async_copy(_token_rows(hp_ref, h * tm + r), _token_rows(xd_ref, d),
                                      sem_rows).start(priority=k)
    for _ in range(2 * DISPATCH_TILES):
        _row_copy_wait(hp_ref, xd_ref, sem_rows, tm)


def _dispatch(pend, dest, hp, n_rows):
    T = hp.shape[0] // ROW_CHUNKS
    tm = TM_ROWS
    grid_spec = pltpu.PrefetchScalarGridSpec(
        num_scalar_prefetch=1,
        grid=(T // (DISPATCH_TILES * tm),),
        in_specs=[pl.BlockSpec(memory_space=pl.ANY),
                  pl.BlockSpec((DISPATCH_TILES * tm * ROW_CHUNKS, LANES), lambda i, pend: (i, 0))],
        out_specs=pl.BlockSpec(memory_space=pl.ANY),
        scratch_shapes=[pltpu.SMEM((2, 2 * DISPATCH_TILES * tm), I32),
                        pltpu.VMEM((TM_EXP * ROW_CHUNKS, LANES), U32),
                        pltpu.SemaphoreType.DMA((2,)), pltpu.SemaphoreType.DMA, pltpu.SemaphoreType.DMA],
    )
    return pl.pallas_call(
        _dispatch_kernel,
        grid_spec=grid_spec,
        out_shape=jax.ShapeDtypeStruct((n_rows * ROW_CHUNKS, LANES), U32),
        compiler_params=_cparams(1),
        name="dispatch",
    )(pend, dest.reshape(-1, 2 * DISPATCH_TILES * tm), hp)


def _experts_kernel(te_ref, nv_ref, xd_ref, wg_ref, wu_ref, wdn_ref, yd_ref, wgu_ref, wd_ref):
    i = pl.program_id(0)
    sub_rows = TM_EXP // EXP_SPLIT
    used = i < nv_ref[0]
    new_expert = jnp.logical_or(i == 0, te_ref[i] != te_ref[jnp.maximum(i - 1, 0)])

    @pl.when(jnp.logical_and(used, new_expert))
    def _():
        wgu_ref[:, :D_EXPERT] = wg_ref[0].astype(BF16)
        wgu_ref[:, D_EXPERT:] = wu_ref[0].astype(BF16)
        wd_ref[...] = wdn_ref[0].astype(BF16)

    def rows_pass(part):
        lo, hi = _unpack_rows(_load_token_rows(xd_ref, part * sub_rows, sub_rows))
        lo, hi = lo.astype(BF16), hi.astype(BF16)
        yield 1
        gu = (jnp.dot(lo, wgu_ref[:HALF, :], preferred_element_type=F32)
              + jnp.dot(hi, wgu_ref[HALF:, :], preferred_element_type=F32))
        yield 2
        gate = gu[:, :D_EXPERT]
        up = gu[:, D_EXPERT:]
        hmid = (gate * jax.nn.sigmoid(gate) * up).astype(BF16)
        yield 3
        y = jnp.dot(hmid, wd_ref[...], preferred_element_type=F32)
        yield 4
        _store_token_rows(yd_ref, part * sub_rows, _pack_rows(y))

    @pl.when(jnp.logical_not(used))
    def _():
        yd_ref[...] = jnp.zeros_like(yd_ref)

    @pl.when(used)
    def _():
        groups = [rows_pass(part) for part in range(EXP_SPLIT)]
        live = True
        step = 0
        while live:
            live = False
            for part, group in enumerate(groups):
                if step >= part and next(group, None) is not None:
                    live = True
            step += 1


def _experts(tile_expert, n_valid, xd, w_gate, w_up, w_down):
    n_tiles = tile_expert.shape[0]
    tm = TM_EXP
    tile = lambda i, te, nv: (jnp.minimum(i, nv[0] - 1), 0)
    weight = lambda i, te, nv: (te[jnp.minimum(i, nv[0] - 1)], 0, 0)
    grid_spec = pltpu.PrefetchScalarGridSpec(
        num_scalar_prefetch=2,
        grid=(n_tiles,),
        in_specs=[pl.BlockSpec((tm * ROW_CHUNKS, LANES), tile),
                  pl.BlockSpec((1, D_MODEL, D_EXPERT), weight),
                  pl.BlockSpec((1, D_MODEL, D_EXPERT), weight),
                  pl.BlockSpec((1, D_EXPERT, D_MODEL), weight)],
        out_specs=pl.BlockSpec((tm * ROW_CHUNKS, LANES), lambda i, te, nv: (i, 0)),
        scratch_shapes=[pltpu.VMEM((D_MODEL, 2 * D_EXPERT), BF16), pltpu.VMEM((D_EXPERT, D_MODEL), BF16)],
    )
    return pl.pallas_call(
        _experts_kernel,
        grid_spec=grid_spec,
        out_shape=jax.ShapeDtypeStruct((n_tiles * tm * ROW_CHUNKS, LANES), U32),
        compiler_params=_cparams(1),
        name="experts",
    )(tile_expert, n_valid, xd, w_gate, w_up, w_down)


def _combine_kernel(dest_ref, yd_ref, x2_ref, ri_ref, g_ref, o_ref, idx_ref, rows_ref, sem_idx, sem_rows):
    tm = TM_ROWS
    j = pl.program_id(0)
    n = pl.num_programs(0)
    last = 2 * n - 1

    def idx_copy(tile, s):
        return pltpu.make_async_copy(dest_ref.at[jnp.minimum(tile, last)], idx_ref.at[s], sem_idx.at[s])

    def issue_rows(s, first, count):
        for r0 in range(first, first + count, ISSUE_BATCH):
            batch = [(r, k, idx_ref[s, k * tm + r]) for r in range(r0, r0 + ISSUE_BATCH) for k in range(2)]
            for r, k, d in batch:
                pltpu.make_async_copy(_token_rows(yd_ref, d), _token_rows(rows_ref.at[s, k], r),
                                      sem_rows.at[s]).start(priority=k)

    def wait_rows(s):
        for k in range(2):
            _row_copy_wait(yd_ref, rows_ref.at[s, k], sem_rows.at[s], tm)

    def consume(s, tile):
        idx_copy(tile + 1, 1 - s).wait()
        wait_rows(s)
        idx_copy(tile + 2, s).start()
        crows = tm // COMBINE_CHUNKS
        issuing = COMBINE_CHUNKS // 2
        per_chunk = tm // issuing
        for c in range(COMBINE_CHUNKS):
            rows = slice(s * tm + c * crows, s * tm + (c + 1) * crows)
            ri = ri_ref[rows, :]
            y_lo = jnp.zeros((crows, HALF), F32)
            y_hi = jnp.zeros((crows, HALF), F32)
            for k in range(2):
                lo, hi = _unpack_rows(_load_token_rows(rows_ref.at[s, k], c * crows, crows))
                wk = ri[:, 2 + k:3 + k]
                y_lo = y_lo + wk * lo
                y_hi = y_hi + wk * hi
            x3 = x2_ref[rows, :] + jnp.concatenate([y_lo, y_hi], axis=1)
            o_ref[rows, :] = _rms(x3, g_ref[...])
            if c < issuing:
                issue_rows(1 - s, c * per_chunk, per_chunk)

    @pl.when(j == 0)
    def _():
        first_idx = idx_copy(0, 0)
        first_idx.start()
        first_idx.wait()
        issue_rows(0, 0, tm)
        idx_copy(1, 1).start()

    consume(0, 2 * j)
    consume(1, 2 * j + 1)

    @pl.when(j == n - 1)
    def _():
        idx_copy(last, 1).wait()
        wait_rows(0)


def _combine(dest, yd, x2, rinfo, g):
    T = x2.shape[0]
    tm = TM_ROWS
    return pl.pallas_call(
        _combine_kernel,
        grid=(T // (2 * tm),),
        in_specs=[pl.BlockSpec(memory_space=pl.ANY),
                  pl.BlockSpec(memory_space=pl.ANY),
                  pl.BlockSpec((2 * tm, D_MODEL), lambda i: (i, 0)),
                  pl.BlockSpec((2 * tm, LANES), lambda i: (i, 0)),
                  pl.BlockSpec((1, D_MODEL), lambda i: (0, 0))],
        out_specs=pl.BlockSpec((2 * tm, D_MODEL), lambda i: (i, 0)),
        out_shape=jax.ShapeDtypeStruct((T, D_MODEL), F32),
        scratch_shapes=[pltpu.SMEM((2, 2 * tm), I32), pltpu.VMEM((2, 2, tm * ROW_CHUNKS, LANES), U32),
                        pltpu.SemaphoreType.DMA((2,)), pltpu.SemaphoreType.DMA((2,))],
        compiler_params=_cparams(1),
        name="combine",
    )(dest, yd, x2, rinfo, g.reshape(1, D_MODEL))


def _routing_tables(rinfo_t, counts_row, T):
    tm = TM_EXP
    counts = counts_row[RLANE_E:RLANE_E + N_EXPERTS].astype(I32)
    padded = (counts + tm - 1) // tm * tm
    pend = jnp.cumsum(padded)
    pstart = pend - padded
    e = rinfo_t[0:2].astype(I32)
    rank = rinfo_t[4:6].astype(I32)
    ids = jnp.arange(N_EXPERTS, dtype=I32)
    dest = rank + jnp.sum(jnp.where(e[..., None] == ids, pstart, 0), axis=-1)
    n_tiles = (2 * T) // tm + N_EXPERTS
    tile_start = jnp.arange(n_tiles, dtype=I32) * tm
    tile_expert = jnp.minimum(jnp.sum((tile_start[:, None] >= pend[None, :]).astype(I32), axis=1),
                              N_EXPERTS - 1)
    n_valid = (pend[-1:] // tm).astype(I32)
    ntt = T // TM_ROWS
    dest_tiles = dest.reshape(2, ntt, TM_ROWS).transpose(1, 0, 2).reshape(ntt, 2 * TM_ROWS)
    return dest_tiles, tile_expert, n_valid, pend.astype(I32), n_tiles


def kernel(x, mem, ln_mix_g, w_in, sb_out_g, conv_w, conv_b, conv_ln_g, conv_ln_b, w_out, ln_mem_x_g,
           ln_mem_g, w_xq, w_xkv, w_xo, ln_ffn_g, w_group, b_group, w_er, b_er, w_gate, w_up, w_down,
           ln_final_g):
    B, S, D = x.shape
    T = B * S
    assert ln_mix_g.shape[0] == 1, "single-layer pipeline: the output norm is fused into the combine kernel"
    assert D == D_MODEL and S % TS_CONV == 0 and S % TM_PROJ == 0 and T % (2 * TM_ROWS) == 0
    l = 0
    qt4, k5, vt5, u = _inproj(x, ln_mix_g[l], w_in[l].astype(BF16))
    sb = _sb_attention(qt4, k5, vt5, sb_out_g[l])
    cv = _conv_branch(u, conv_w[l], conv_b[l], conv_ln_g[l], conv_ln_b[l])
    kt, vx = _memkv(mem, ln_mem_g[l], w_xkv[l].astype(BF16))

    pad = LANES - RLANE_E - N_EXPERTS
    w_router = jnp.concatenate([w_group[l], w_er[l].reshape(D, N_EXPERTS), jnp.zeros((D, pad), F32)], axis=1)
    b_router = jnp.concatenate([b_group[l], b_er[l].reshape(N_EXPERTS), jnp.zeros((pad,), F32)])
    x2, hp, rinfo, rinfo_t, cnt = _post(x, sb, cv, w_out[l].astype(BF16), ln_mem_x_g[l], w_xq[l].astype(BF16),
                               kt, vx, w_xo[l].astype(BF16), ln_ffn_g[l],
                               w_router.astype(BF16), b_router.reshape(1, LANES))

    dest_tiles, tile_expert, n_valid, pend, n_tiles = _routing_tables(rinfo_t, cnt[0], T)
    xd = _dispatch(pend, dest_tiles, hp, n_tiles * TM_EXP)
    yd = _experts(tile_expert, n_valid, xd, w_gate[l], w_up[l], w_down[l])
    return _combine(dest_tiles, yd, x2.reshape(T, D), rinfo, ln_final_g).reshape(B, S, D)
```
